```python
import math
import jax, jax.numpy as jnp
from jax import lax
import numpy as np

D_MODEL = 1024
BATCH = 16
SEQ = 2048
DEPTH = 4

RWKV_WIDTH = D_MODEL // 2
RWKV_HEAD = 64
RWKV_HEADS = RWKV_WIDTH // RWKV_HEAD
DECAY_LORA = 64
AAA_LORA = 64
GATE_LORA = 128
LNX_EPS = 64e-5
DECAY_SCALE = math.exp(-0.5)
SSM_WIDTH = D_MODEL // 2
SSM_GROUP = 16
SSM_GROUPS = SSM_WIDTH // SSM_GROUP
SSM_STATE = 64
D_FF = 4 * D_MODEL
NORM_EPS = 1e-6
SHIFT_COLS = 3 * RWKV_WIDTH + DECAY_LORA + AAA_LORA + GATE_LORA
IN_COLS = SHIFT_COLS + SSM_WIDTH + 2 * D_MODEL

kernel_name = "rwkv7_s5_gated_hybrid"


def rmsnorm(x, g):
    xf = x.astype(jnp.float32)
    y = xf * lax.rsqrt(jnp.mean(xf * xf, axis=-1, keepdims=True) + NORM_EPS)
    return (y * g.astype(jnp.float32)).astype(x.dtype)


def token_shift(z):
    return jnp.pad(z[:, :-1], ((0, 0), (1, 0), (0, 0)))


def rwkv7_time_mix(zm, k_k, k_a, r_k, w0, w_decay_up, a0, w_aaa_up, w_gate_up,
                   lnx_w, lnx_b, w_proj):
    bsz, seq, _ = zm.shape
    W, H, N = RWKV_WIDTH, RWKV_HEADS, RWKV_HEAD
    f32 = jnp.float32
    r, k, v, wd, ad, gd = jnp.split(
        zm, [W, 2 * W, 3 * W, 3 * W + DECAY_LORA, 3 * W + DECAY_LORA + AAA_LORA], axis=-1)
    w = jnp.exp(-DECAY_SCALE * jax.nn.sigmoid((w0 + jnp.tanh(wd) @ w_decay_up).astype(f32)))
    a = jax.nn.sigmoid(a0 + ad @ w_aaa_up)
    g = jax.nn.sigmoid(gd) @ w_gate_up
    kk = k * k_k
    k_mod = k * (1.0 + (a - 1.0) * k_a)

    heads = lambda t: t.reshape(bsz, seq, H, N).astype(f32)
    rh, kh, vh, wh, ah, kkh = map(heads, (r, k_mod, v, w, a, kk))
    kkh = kkh / jnp.maximum(jnp.sqrt(jnp.sum(kkh * kkh, axis=-1, keepdims=True)), 1e-12)
    tm = lambda t: jnp.swapaxes(t, 0, 1)

    def step(state, inp):
        r_t, w_t, k_t, v_t, kk_t, b_t = inp
        sk = jnp.einsum('bhvk,bhk->bhv', state, kk_t)
        state = (state * w_t[:, :, None, :]
                 - sk[..., None] * b_t[:, :, None, :]
                 + v_t[..., None] * k_t[:, :, None, :])
        return state, jnp.einsum('bhvk,bhk->bhv', state, r_t)

    state0 = jnp.zeros((bsz, H, N, N), f32)
    _, y = lax.scan(step, state0, tuple(map(tm, (rh, wh, kh, vh, kkh, kkh * ah))))
    y = tm(y)
    mu = jnp.mean(y, axis=-1, keepdims=True)
    var = jnp.mean(jnp.square(y - mu), axis=-1, keepdims=True)
    y = ((y - mu) * lax.rsqrt(var + LNX_EPS)).reshape(bsz, seq, W)
    y = y * lnx_w.astype(f32) + lnx_b.astype(f32)
    bonus = (jnp.sum(rh * kh * r_k.astype(f32), axis=-1, keepdims=True) * vh).reshape(bsz, seq, W)
    y = (y + bonus).astype(zm.dtype) * g
    return y @ w_proj


def _complex_scan_op(e1, e2):
    a1r, a1i, b1r, b1i = e1
    a2r, a2i, b2r, b2i = e2
    return (a2r * a1r - a2i * a1i,
            a2r * a1i + a2i * a1r,
            a2r * b1r - a2i * b1i + b2r,
            a2r * b1i + a2i * b1r + b2i)


def s5_ssm(u, a_re, a_im, log_dt, b_re, b_im, c_re, c_im, d_skip, w_glu, b_glu):
    bsz, seq, _ = u.shape
    G, C, P = SSM_GROUPS, SSM_GROUP, SSM_STATE
    f32 = jnp.float32
    uf = u.astype(f32).reshape(bsz, seq, G, C)
    dt = jnp.exp(log_dt.astype(f32))[:, None]
    are = jnp.minimum(a_re.astype(f32), -1e-4)
    aim = a_im.astype(f32)
    mag = jnp.exp(dt * are)
    abar_re = mag * jnp.cos(dt * aim)
    abar_im = mag * jnp.sin(dt * aim)
    den = are * are + aim * aim
    nr = abar_re - 1.0
    coef_re = (nr * are + abar_im * aim) / den
    coef_im = (abar_im * are - nr * aim) / den
    br, bi = b_re.astype(f32), b_im.astype(f32)
    bbar_re = coef_re[..., None] * br - coef_im[..., None] * bi
    bbar_im = coef_re[..., None] * bi + coef_im[..., None] * br
    bu_re = jnp.einsum('gpc,bsgc->bsgp', bbar_re, uf)
    bu_im = jnp.einsum('gpc,bsgc->bsgp', bbar_im, uf)
    a_seq_re = jnp.broadcast_to(abar_re, (1, seq, G, P))
    a_seq_im = jnp.broadcast_to(abar_im, (1, seq, G, P))
    _, _, xr, xi = lax.associative_scan(_complex_scan_op, (a_seq_re, a_seq_im, bu_re, bu_im), axis=1)
    y = (jnp.einsum('gcp,bsgp->bsgc', c_re.astype(f32), xr)
         - jnp.einsum('gcp,bsgp->bsgc', c_im.astype(f32), xi)
         + d_skip.astype(f32).reshape(G, C) * uf)
    y = jax.nn.gelu(y.reshape(bsz, seq, SSM_WIDTH), approximate=False).astype(u.dtype)
    z = y @ w_glu + b_glu
    za, zb = jnp.split(z, 2, axis=-1)
    return za * jax.nn.sigmoid(zb)


def sq_relu_mlp(x, w_up, w_down):
    return jnp.square(jax.nn.relu(x @ w_up)) @ w_down


def setup_inputs(seed: int = 0) -> dict:
    key = jax.random.key(seed)
    ks = iter(jax.random.split(key, 40))
    L, D, W, H, N = DEPTH, D_MODEL, RWKV_WIDTH, RWKV_HEADS, RWKV_HEAD
    G, C, P = SSM_GROUPS, SSM_GROUP, SSM_STATE
    nrm = lambda shape, s: jax.random.normal(next(ks), shape, jnp.float32) * s
    unif = lambda shape, lo, hi: jax.random.uniform(next(ks), shape, jnp.float32, lo, hi)
    return {
        "x": nrm((BATCH, SEQ, D), 1.0),
        "norm1_g": 1.0 + nrm((L, D), 0.02),
        "w_in": nrm((L, D, IN_COLS), D ** -0.5),
        "mu_shift": unif((L, SHIFT_COLS), 0.0, 1.0),
        "k_k": 0.85 + nrm((L, W), 0.05),
        "k_a": 1.0 + nrm((L, W), 0.05),
        "r_k": nrm((L, H, N), 0.1),
        "w0": unif((L, W), -3.0, 3.0),
        "w_decay_up": nrm((L, DECAY_LORA, W), 0.1),
        "a0": nrm((L, W), 0.1),
        "w_aaa_up": nrm((L, AAA_LORA, W), 0.1),
        "w_gate_up": nrm((L, GATE_LORA, W), GATE_LORA ** -0.5),
        "lnx_w": 1.0 + nrm((L, W), 0.02),
        "lnx_b": nrm((L, W), 0.01),
        "w_rwkv_proj": nrm((L, W, D), W ** -0.5),
        "a_re": -0.5 + nrm((L, G, P), 0.01),
        "a_im": jnp.pi * jnp.arange(P, dtype=jnp.float32) + nrm((L, G, P), 0.01),
        "log_dt": unif((L, G), math.log(1e-3), math.log(1e-1)),
        "b_re": nrm((L, G, P, C), (2 * C) ** -0.5),
        "b_im": nrm((L, G, P, C), (2 * C) ** -0.5),
        "c_re": nrm((L, G, C, P), (2 * P) ** -0.5),
        "c_im": nrm((L, G, C, P), (2 * P) ** -0.5),
        "d_skip": nrm((L, SSM_WIDTH), 1.0),
        "w_glu": nrm((L, SSM_WIDTH, 2 * D), SSM_WIDTH ** -0.5),
        "b_glu": nrm((L, 2 * D), 0.01),
        "w_out": nrm((L, D, D), D ** -0.5),
        "norm2_g": 1.0 + nrm((L, D), 0.02),
        "w_ff_up": nrm((L, D, D_FF), D ** -0.5),
        "w_ff_down": nrm((L, D_FF, D), D_FF ** -0.5),
        "norm_f_g": 1.0 + nrm((D,), 0.02),
    }


def reference(x, norm1_g, w_in, mu_shift, k_k, k_a, r_k, w0, w_decay_up, a0, w_aaa_up,
              w_gate_up, lnx_w, lnx_b, w_rwkv_proj, a_re, a_im, log_dt, b_re, b_im, c_re,
              c_im, d_skip, w_glu, b_glu, w_out, norm2_g, w_ff_up, w_ff_down, norm_f_g):
    for l in range(DEPTH):
        xn = rmsnorm(x, norm1_g[l])
        z = xn @ w_in[l]
        z_rwkv, u, gates = jnp.split(z, [SHIFT_COLS, SHIFT_COLS + SSM_WIDTH], axis=-1)
        z_mix = z_rwkv + (token_shift(z_rwkv) - z_rwkv) * mu_shift[l]
        y_a = rwkv7_time_mix(z_mix, k_k[l], k_a[l], r_k[l], w0[l], w_decay_up[l], a0[l],
                             w_aaa_up[l], w_gate_up[l], lnx_w[l], lnx_b[l], w_rwkv_proj[l])
        y_b = s5_ssm(u, a_re[l], a_im[l], log_dt[l], b_re[l], b_im[l], c_re[l], c_im[l],
                     d_skip[l], w_glu[l], b_glu[l])
        g_a, g_b = jnp.split(jax.nn.sigmoid(gates), 2, axis=-1)
        x = x + (g_a * y_a + g_b * y_b) @ w_out[l]
        x = x + sq_relu_mlp(rmsnorm(x, norm2_g[l]), w_ff_up[l], w_ff_down[l])
    return rmsnorm(x, norm_f_g)
```

```python
import functools
import math

import jax
import jax.numpy as jnp
from jax import lax
from jax.experimental import pallas as pl
from jax.experimental.pallas import tpu as pltpu

F32 = jnp.float32
BF16 = jnp.bfloat16

RWKV_HEAD = 64
DECAY_LORA = 64
AAA_LORA = 64
GATE_LORA = 128
LNX_EPS = 64e-5
DECAY_SCALE = math.exp(-0.5)
SSM_GROUP = 16
SSM_STATE = 64
NORM_EPS = 1e-6

CHUNK = 64
VMEM_LIMIT = 56 * 1024 * 1024


def _dotb(a, b):
    return jnp.dot(a.astype(BF16), b.astype(BF16), preferred_element_type=F32)


def _dotf(a, b):
    return jnp.dot(a, b, preferred_element_type=F32, precision=lax.Precision.HIGHEST)


def _dotf_nt(a, b):
    return lax.dot_general(a, b, (((1,), (1,)), ((), ())), preferred_element_type=F32,
                           precision=lax.Precision.HIGHEST)


def _dotf_tn(a, b):
    return lax.dot_general(a, b, (((0,), (0,)), ((), ())), preferred_element_type=F32,
                           precision=lax.Precision.HIGHEST)


def _iota2(shape, axis):
    return lax.broadcasted_iota(jnp.int32, shape, axis)


def _in_proj_kernel(x_ref, g_ref, w_ref, zr_ref, u_ref, gt_ref, *, n_shift, n_ssm):
    x = x_ref[...]
    xn = x * lax.rsqrt(jnp.mean(x * x, axis=-1, keepdims=True) + NORM_EPS) * g_ref[...]
    z = _dotb(xn, w_ref[...])
    zr_ref[...] = z[:, :n_shift]
    u_ref[...] = z[:, n_shift:n_shift + n_ssm]
    gt_ref[...] = z[:, n_shift + n_ssm:]


def _in_proj(x2, g, w_bf, n_shift, n_ssm, tm=256):
    t, d = x2.shape
    n_in = w_bf.shape[1]
    n_gate = n_in - n_shift - n_ssm
    return pl.pallas_call(
        functools.partial(_in_proj_kernel, n_shift=n_shift, n_ssm=n_ssm),
        grid=(t // tm,),
        in_specs=[pl.BlockSpec((tm, d), lambda i: (i, 0)),
                  pl.BlockSpec((1, d), lambda i: (0, 0)),
                  pl.BlockSpec((d, n_in), lambda i: (0, 0))],
        out_specs=[pl.BlockSpec((tm, n_shift), lambda i: (i, 0)),
                   pl.BlockSpec((tm, n_ssm), lambda i: (i, 0)),
                   pl.BlockSpec((tm, n_gate), lambda i: (i, 0))],
        out_shape=[jax.ShapeDtypeStruct((t, n_shift), F32),
                   jax.ShapeDtypeStruct((t, n_ssm), F32),
                   jax.ShapeDtypeStruct((t, n_gate), F32)],
        compiler_params=pltpu.CompilerParams(dimension_semantics=("parallel",),
                                             vmem_limit_bytes=VMEM_LIMIT),
        name="in_proj",
    )(x2, g, w_bf)


def _unit_lower_inverse(n_strict):
    L = n_strict.shape[0]
    ri = _iota2((L, L), 0)
    ci = _iota2((L, L), 1)
    eye = (ri == ci).astype(F32)
    same8 = (ri // 8) == (ci // 8)
    n0 = jnp.where(same8, n_strict, 0.0)
    x = eye - n0
    p = _dotf(n0, n0)
    x = x + _dotf(x, p)
    p = _dotf(p, p)
    x = x + _dotf(x, p)
    blk = 16
    while blk <= L:
        sel = ((ri // blk) == (ci // blk)) & ((ri // (blk // 2)) != (ci // (blk // 2)))
        c = jnp.where(sel, n_strict, 0.0)
        x = x - _dotf(_dotf(x, c), x)
        blk *= 2
    return x


def _rwkv_kernel(z_ref, mu_ref, kk_ref, ka_ref, rk_ref, w0_ref, wdu_ref, a0_ref, wau_ref,
                 wgu_ref, lnw_ref, lnb_ref, wproj_ref, out_ref, carry_ref, state_ref, y_scr,
                 *, width, heads):
    rows = z_ref.shape[1]
    n = RWKV_HEAD
    L = CHUNK
    nchunks = rows // L

    @pl.when(pl.program_id(1) == 0)
    def _():
        carry_ref[...] = jnp.zeros_like(carry_ref)
        state_ref[...] = jnp.zeros_like(state_ref)

    z = z_ref[0]
    row_id = _iota2((rows, 1), 0)
    zprev = jnp.where(row_id == 0, carry_ref[0:1, :], pltpu.roll(z, 1, axis=0))
    carry_ref[0:1, :] = z[rows - 1:rows, :]
    zm = z + (zprev - z) * mu_ref[...]

    W = width
    r = zm[:, 0:W]
    k = zm[:, W:2 * W]
    v = zm[:, 2 * W:3 * W]
    wd = zm[:, 3 * W:3 * W + DECAY_LORA]
    ad = zm[:, 3 * W + DECAY_LORA:3 * W + DECAY_LORA + AAA_LORA]
    gd = zm[:, 3 * W + DECAY_LORA + AAA_LORA:]

    lw = -DECAY_SCALE * jax.nn.sigmoid(w0_ref[...] + _dotb(jnp.tanh(wd), wdu_ref[...]))
    a = jax.nn.sigmoid(a0_ref[...] + _dotb(ad, wau_ref[...]))
    g = _dotb(jax.nn.sigmoid(gd), wgu_ref[...])

    seg = ((_iota2((W, W), 0) // n) == (_iota2((W, W), 1) // n)).astype(F32)
    kk = k * kk_ref[...]
    kk_norm = jnp.sqrt(_dotf(kk * kk, seg))
    kk = kk / jnp.maximum(kk_norm, 1e-12)
    k_mod = k * (1.0 + (a - 1.0) * ka_ref[...])
    b = kk * a

    tri_incl = (_iota2((L, L), 0) >= _iota2((L, L), 1))
    tri_strict = (_iota2((L, L), 0) > _iota2((L, L), 1))
    tri_f = tri_incl.astype(F32)
    eye_n = (_iota2((n, n), 0) == _iota2((n, n), 1)).astype(F32)

    for c in range(nchunks):
        sl = slice(c * L, (c + 1) * L)
        lw_c = lw[sl]
        cum = _dotf(tri_f, lw_c)
        cum_last = cum[L - 1:L, :]
        g_in = jnp.exp(cum)
        g_ex = jnp.exp(cum - lw_c)
        g_inv = jnp.exp(-cum)
        g_end = jnp.exp(cum_last - cum)
        g_last = jnp.exp(cum_last)
        rt_all = r[sl] * g_in
        at_all = kk[sl] * g_ex
        kt_all = k_mod[sl] * g_inv
        bt_all = b[sl] * g_inv
        kh_all = k_mod[sl] * g_end
        bh_all = b[sl] * g_end
        v_all = v[sl]
        for h in range(heads):
            hs = slice(h * n, (h + 1) * n)
            rt, at, kt, bt = rt_all[:, hs], at_all[:, hs], kt_all[:, hs], bt_all[:, hs]
            kh, bh, vv = kh_all[:, hs], bh_all[:, hs], v_all[:, hs]
            n_ab = jnp.where(tri_strict, _dotf_nt(at, bt), 0.0)
            n_ak = jnp.where(tri_strict, _dotf_nt(at, kt), 0.0)
            m_rb = jnp.where(tri_incl, _dotf_nt(rt, bt), 0.0)
            m_rk = jnp.where(tri_incl, _dotf_nt(rt, kt), 0.0)
            t_inv = _unit_lower_inverse(n_ab)
            p = _dotf(t_inv, at)
            q = _dotf(t_inv, _dotf(n_ak, vv))
            y_a = rt - _dotf(m_rb, p)
            y_b = _dotf(m_rk, vv) - _dotf(m_rb, q)
            m_z = eye_n * g_last[:, hs] - _dotf_tn(bh, p)
            n_z = _dotf_tn(kh, vv) - _dotf_tn(bh, q)
            zst = state_ref[h]
            y_scr[sl, hs] = _dotf(y_a, zst) + y_b
            state_ref[h] = _dotf(m_z, zst) + n_z

    y = y_scr[...]
    inv_n = 1.0 / n
    mean = _dotf(y, seg) * inv_n
    yc = y - mean
    var = _dotf(yc * yc, seg) * inv_n
    yn = yc * lax.rsqrt(var + LNX_EPS) * lnw_ref[...] + lnb_ref[...]
    bonus = _dotf(r * k_mod * rk_ref[...], seg) * v
    out_ref[0] = _dotb((yn + bonus) * g, wproj_ref[...])


def _rwkv_mix(z_rwkv, mu, k_k, k_a, r_k, w0, wdu, a0, wau, wgu, lnw, lnb, wproj, rows=128):
    bsz, seq, ncols = z_rwkv.shape
    W = k_k.shape[1]
    heads = W // RWKV_HEAD
    d = wproj.shape[1]
    full = lambda arr: pl.BlockSpec(arr.shape, lambda b, j: (0,) * arr.ndim)
    params = (mu, k_k, k_a, r_k, w0, wdu, a0, wau, wgu, lnw, lnb, wproj)
    return pl.pallas_call(
        functools.partial(_rwkv_kernel, width=W, heads=heads),
        grid=(bsz, seq // rows),
        in_specs=[pl.BlockSpec((1, rows, ncols), lambda b, j: (b, j, 0))] + [full(p) for p in params],
        out_specs=pl.BlockSpec((1, rows, d), lambda b, j: (b, j, 0)),
        out_shape=jax.ShapeDtypeStruct((bsz, seq, d), F32),
        scratch_shapes=[pltpu.VMEM((8, ncols), F32),
                        pltpu.VMEM((heads, RWKV_HEAD, RWKV_HEAD), F32),
                        pltpu.VMEM((rows, W), F32)],
        compiler_params=pltpu.CompilerParams(dimension_semantics=("parallel", "arbitrary"),
                                             vmem_limit_bytes=VMEM_LIMIT),
        name="rwkv_mix",
    )(z_rwkv, *params)


def _s5_disc_kernel(are_ref, aim_ref, ldt_ref, bre_ref, bim_ref, abr_ref, abi_ref, bbr_ref, bbi_ref):
    dt = jnp.exp(ldt_ref[...])
    are = jnp.minimum(are_ref[...], -1e-4)
    aim = aim_ref[...]
    mag = jnp.exp(dt * are)
    abr = mag * jnp.cos(dt * aim)
    abi = mag * jnp.sin(dt * aim)
    den = are * are + aim * aim
    nr = abr - 1.0
    cre = (nr * are + abi * aim) / den
    cim = (abi * are - nr * aim) / den
    abr_ref[...] = abr
    abi_ref[...] = abi
    br = bre_ref[...]
    bi = bim_ref[...]
    bbr_ref[...] = cre[None] * br - cim[None] * bi
    bbi_ref[...] = cre[None] * bi + cim[None] * br


def _s5_disc(a_re, a_im, log_dt, b_re_c, b_im_c):
    G, P = a_re.shape
    C = b_re_c.shape[0]
    return pl.pallas_call(
        _s5_disc_kernel,
        out_shape=[jax.ShapeDtypeStruct((G, P), F32), jax.ShapeDtypeStruct((G, P), F32),
                   jax.ShapeDtypeStruct((C, G, P), F32), jax.ShapeDtypeStruct((C, G, P), F32)],
        name="s5_disc",
    )(a_re, a_im, log_dt.reshape(G, 1), b_re_c, b_im_c)


def _s5_kernel(u_ref, bmat_ref, cmat_ref, apow_r_ref, apow_i_ref, dskip_ref, wglu_ref, bglu_ref,
               out_ref, st_r_ref, st_i_ref, x_scr, *, nslab):
    rows = u_ref.shape[1]
    ns = apow_r_ref.shape[1]
    sw = ns // nslab
    d = out_ref.shape[2]

    @pl.when(pl.program_id(1) == 0)
    def _():
        st_r_ref[...] = jnp.zeros_like(st_r_ref)
        st_i_ref[...] = jnp.zeros_like(st_i_ref)

    u = u_ref[0]
    sub = _iota2((8, 1), 0)
    ys = []
    for s in range(nslab):
        us = u[:, s * 128:(s + 1) * 128]
        bu = _dotb(us, bmat_ref[s])
        cs = slice(s * sw, (s + 1) * sw)
        ar = apow_r_ref[:, cs]
        ai = apow_i_ref[:, cs]
        a1r, a1i = ar[0:1], ai[0:1]
        a2r, a2i = ar[1:2], ai[1:2]
        a4r, a4i = ar[3:4], ai[3:4]
        cr = st_r_ref[:, cs]
        cim = st_i_ref[:, cs]
        for blk in range(rows // 8):
            xr = bu[blk * 8:(blk + 1) * 8, :sw]
            xi = bu[blk * 8:(blk + 1) * 8, sw:]
            for sh, (pr, pi) in ((1, (a1r, a1i)), (2, (a2r, a2i)), (4, (a4r, a4i))):
                sr = jnp.where(sub >= sh, pltpu.roll(xr, sh, axis=0), 0.0)
                si = jnp.where(sub >= sh, pltpu.roll(xi, sh, axis=0), 0.0)
                xr, xi = xr + pr * sr - pi * si, xi + pr * si + pi * sr
            xr, xi = xr + ar * cr - ai * cim, xi + ar * cim + ai * cr
            cr, cim = xr[7:8], xi[7:8]
            x_scr[blk * 8:(blk + 1) * 8, :sw] = xr
            x_scr[blk * 8:(blk + 1) * 8, sw:] = xi
        st_r_ref[:, cs] = cr
        st_i_ref[:, cs] = cim
        ys.append(_dotb(x_scr[...], cmat_ref[s]))
    y = jnp.concatenate(ys, axis=-1) + dskip_ref[...] * u
    y = 0.5 * y * (1.0 + lax.erf(y * (1.0 / math.sqrt(2.0))))
    zz = _dotb(y, wglu_ref[...]) + bglu_ref[...]
    out_ref[0] = zz[:, :d] * jax.nn.sigmoid(zz[:, d:])


def _s5_mix(u3, bmat, cmat, apow_r, apow_i, dskip, wglu, bglu, rows=128):
    bsz, seq, wc = u3.shape
    nslab = bmat.shape[0]
    ns = apow_r.shape[1]
    d = wglu.shape[1] // 2
    full = lambda arr: pl.BlockSpec(arr.shape, lambda b, j: (0,) * arr.ndim)
    params = (bmat, cmat, apow_r, apow_i, dskip, wglu, bglu)
    return pl.pallas_call(
        functools.partial(_s5_kernel, nslab=nslab),
        grid=(bsz, seq // rows),
        in_specs=[pl.BlockSpec((1, rows, wc), lambda b, j: (b, j, 0))] + [full(p) for p in params],
        out_specs=pl.BlockSpec((1, rows, d), lambda b, j: (b, j, 0)),
        out_shape=jax.ShapeDtypeStruct((bsz, seq, d), F32),
        scratch_shapes=[pltpu.VMEM((1, ns), F32), pltpu.VMEM((1, ns), F32),
                        pltpu.VMEM((rows, 2 * ns // nslab), F32)],
        compiler_params=pltpu.CompilerParams(dimension_semantics=("parallel", "arbitrary"),
                                             vmem_limit_bytes=VMEM_LIMIT),
        name="s5_mix",
    )(u3, *params)


def _s5_pow_kernel(abr_ref, abi_ref, pr_ref, pi_ref):
    ar, ai = abr_ref[...], abi_ref[...]
    cr, ci = ar, ai
    for j in range(8):
        pr_ref[j:j + 1, :] = cr
        pi_ref[j:j + 1, :] = ci
        cr, ci = cr * ar - ci * ai, cr * ai + ci * ar


def _s5_pow(abr_flat, abi_flat):
    ns = abr_flat.shape[1]
    return pl.pallas_call(
        _s5_pow_kernel,
        out_shape=[jax.ShapeDtypeStruct((8, ns), F32), jax.ShapeDtypeStruct((8, ns), F32)],
        name="s5_pow",
    )(abr_flat, abi_flat)


def _merge_ffn_kernel(x_ref, gt_ref, ya_ref, yb_ref, wout_ref, g2_ref, wup_ref, wdn_ref, gf_ref,
                      out_ref, *, final_norm, ff_chunk):
    d = x_ref.shape[1]
    gt = jax.nn.sigmoid(gt_ref[...])
    m = gt[:, :d] * ya_ref[...] + gt[:, d:] * yb_ref[...]
    x1 = x_ref[...] + _dotb(m, wout_ref[...])
    xn = x1 * lax.rsqrt(jnp.mean(x1 * x1, axis=-1, keepdims=True) + NORM_EPS) * g2_ref[...]
    xn = xn.astype(BF16)
    acc = x1
    dff = wup_ref.shape[1]
    for c in range(dff // ff_chunk):
        cs = slice(c * ff_chunk, (c + 1) * ff_chunk)
        h = jnp.dot(xn, wup_ref[:, cs], preferred_element_type=F32)
        h = jnp.square(jnp.maximum(h, 0.0))
        acc = acc + _dotb(h, wdn_ref[cs, :])
    if final_norm:
        acc = acc * lax.rsqrt(jnp.mean(acc * acc, axis=-1, keepdims=True) + NORM_EPS) * gf_ref[...]
    out_ref[...] = acc


def _merge_ffn(x2, gates, ya, yb, wout, g2, wup, wdn, gf, final_norm, tm=256, ff_chunk=1024):
    t, d = x2.shape
    dff = wup.shape[1]
    row = lambda w: pl.BlockSpec((tm, w), lambda i: (i, 0))
    full = lambda arr: pl.BlockSpec(arr.shape, lambda i: (0,) * arr.ndim)
    return pl.pallas_call(
        functools.partial(_merge_ffn_kernel, final_norm=final_norm, ff_chunk=ff_chunk),
        grid=(t // tm,),
        in_specs=[row(d), row(2 * d), row(d), row(d), full(wout), full(g2), full(wup), full(wdn), full(gf)],
        out_specs=row(d),
        out_shape=jax.ShapeDtypeStruct((t, d), F32),
        compiler_params=pltpu.CompilerParams(dimension_semantics=("parallel",),
                                             vmem_limit_bytes=VMEM_LIMIT),
        name="merge_ffn",
    )(x2, gates, ya, yb, wout, g2, wup, wdn, gf)


def _s5_layer_params(a_re, a_im, log_dt, b_re, b_im, c_re, c_im):
    G, P, C = b_re.shape
    gps = 128 // C
    nslab = G // gps
    abr, abi, bbr, bbi = _s5_disc(a_re, a_im, log_dt, jnp.transpose(b_re, (2, 0, 1)),
                                  jnp.transpose(b_im, (2, 0, 1)))
    eye = jnp.eye(gps, dtype=F32)

    def bd_in(bb):
        bb = jnp.transpose(bb, (1, 0, 2)).reshape(nslab, gps, C, P)
        return jnp.einsum('sgcp,gh->sgchp', bb, eye).reshape(nslab, gps * C, gps * P)

    def bd_out(cc):
        cc = cc.reshape(nslab, gps, C, P)
        return jnp.einsum('sgcp,gh->sgphc', cc, eye).reshape(nslab, gps * P, gps * C)

    bmat = jnp.concatenate([bd_in(bbr), bd_in(bbi)], axis=2).astype(BF16)
    cmat = jnp.concatenate([bd_out(c_re), -bd_out(c_im)], axis=1).astype(BF16)
    apow_r, apow_i = _s5_pow(abr.reshape(1, G * P), abi.reshape(1, G * P))
    return bmat, cmat, apow_r, apow_i


def kernel(x, norm1_g, w_in, mu_shift, k_k, k_a, r_k, w0, w_decay_up, a0, w_aaa_up, w_gate_up,
           lnx_w, lnx_b, w_rwkv_proj, a_re, a_im, log_dt, b_re, b_im, c_re, c_im, d_skip, w_glu,
           b_glu, w_out, norm2_g, w_ff_up, w_ff_down, norm_f_g):
    bsz, seq, d = x.shape
    depth = w_in.shape[0]
    W = k_k.shape[1]
    n_shift = mu_shift.shape[1]
    n_ssm = d_skip.shape[1]
    t = bsz * seq
    row = lambda vec: vec.reshape(1, -1)
    x2 = x.reshape(t, d)
    for l in range(depth):
        z_rwkv, u, gates = _in_proj(x2, row(norm1_g[l]), w_in[l].astype(BF16), n_shift, n_ssm)
        y_a = _rwkv_mix(z_rwkv.reshape(bsz, seq, n_shift), row(mu_shift[l]), row(k_k[l]), row(k_a[l]),
                        row(r_k[l]), row(w0[l]), w_decay_up[l].astype(BF16), row(a0[l]),
                        w_aaa_up[l].astype(BF16), w_gate_up[l].astype(BF16), row(lnx_w[l]),
                        row(lnx_b[l]), w_rwkv_proj[l].astype(BF16))
        bmat, cmat, apow_r, apow_i = _s5_layer_params(a_re[l], a_im[l], log_dt[l], b_re[l], b_im[l],
                                                      c_re[l], c_im[l])
        y_b = _s5_mix(u.reshape(bsz, seq, n_ssm), bmat, cmat, apow_r, apow_i, row(d_skip[l]),
                      w_glu[l].astype(BF16), row(b_glu[l]))
        x2 = _merge_ffn(x2, gates, y_a.reshape(t, d), y_b.reshape(t, d), w_out[l].astype(BF16),
                        row(norm2_g[l]), w_ff_up[l].astype(BF16), w_ff_down[l].astype(BF16),
                        row(norm_f_g), final_norm=(l == depth - 1))
    return x2.reshape(bsz, seq, d)
```

```python
import functools
import math

import jax
import jax.numpy as jnp
from jax import lax
from jax.experimental import pallas as pl
from jax.experimental.pallas import tpu as pltpu

F32 = jnp.float32
BF16 = jnp.bfloat16

RWKV_HEAD = 64
DECAY_LORA = 64
AAA_LORA = 64
GATE_LORA = 128
LNX_EPS = 64e-5
DECAY_SCALE = math.exp(-0.5)
SSM_GROUP = 16
SSM_STATE = 64
NORM_EPS = 1e-6

CHUNK = 64
VMEM_LIMIT = 56 * 1024 * 1024


def _dotb(a, b):
    return jnp.dot(a.astype(BF16), b.astype(BF16), preferred_element_type=F32)


def _dotf(a, b):
    return jnp.dot(a, b, preferred_element_type=F32, precision=lax.Precision.HIGHEST)


def _dotf_nt(a, b):
    return lax.dot_general(a, b, (((1,), (1,)), ((), ())), preferred_element_type=F32,
                           precision=lax.Precision.HIGHEST)


def _dotf_tn(a, b):
    return lax.dot_general(a, b, (((0,), (0,)), ((), ())), preferred_element_type=F32,
                           precision=lax.Precision.HIGHEST)


def _iota2(shape, axis):
    return lax.broadcasted_iota(jnp.int32, shape, axis)


def _in_proj_kernel(x_ref, g_ref, w_ref, zr_ref, u_ref, gt_ref, *, n_shift, n_ssm):
    x = x_ref[...]
    xn = x * lax.rsqrt(jnp.mean(x * x, axis=-1, keepdims=True) + NORM_EPS) * g_ref[...]
    z = _dotb(xn, w_ref[...])
    zr_ref[...] = z[:, :n_shift]
    u_ref[...] = z[:, n_shift:n_shift + n_ssm]
    gt_ref[...] = z[:, n_shift + n_ssm:]


def _in_proj(x2, g, w_bf, n_shift, n_ssm, tm=256):
    t, d = x2.shape
    n_in = w_bf.shape[1]
    n_gate = n_in - n_shift - n_ssm
    return pl.pallas_call(
        functools.partial(_in_proj_kernel, n_shift=n_shift, n_ssm=n_ssm),
        grid=(t // tm,),
        in_specs=[pl.BlockSpec((tm, d), lambda i: (i, 0)),
                  pl.BlockSpec((1, d), lambda i: (0, 0)),
                  pl.BlockSpec((d, n_in), lambda i: (0, 0))],
        out_specs=[pl.BlockSpec((tm, n_shift), lambda i: (i, 0)),
                   pl.BlockSpec((tm, n_ssm), lambda i: (i, 0)),
                   pl.BlockSpec((tm, n_gate), lambda i: (i, 0))],
        out_shape=[jax.ShapeDtypeStruct((t, n_shift), F32),
                   jax.ShapeDtypeStruct((t, n_ssm), F32),
                   jax.ShapeDtypeStruct((t, n_gate), F32)],
        compiler_params=pltpu.CompilerParams(dimension_semantics=("parallel",),
                                             vmem_limit_bytes=VMEM_LIMIT),
        name="in_proj",
    )(x2, g, w_bf)


def _dot_exact(m_bf, x, terms, *, m_left):
    acc = None
    rem = x
    for _ in range(terms):
        piece = rem.astype(BF16)
        d = (jnp.dot(m_bf, piece, preferred_element_type=F32) if m_left
             else jnp.dot(piece, m_bf, preferred_element_type=F32))
        acc = d if acc is None else acc + d
        rem = rem - piece.astype(F32)
    return acc


def _unit_lower_inverse(ns):
    L = ns[0].shape[0]
    ri = _iota2((L, L), 0)
    ci = _iota2((L, L), 1)
    eye = (ri == ci).astype(F32)
    same8 = (ri >> 3) == (ci >> 3)
    n0 = [jnp.where(same8, m, 0.0) for m in ns]
    x = [eye - m for m in n0]
    n0 = [m.astype(BF16) for m in n0]
    p = [_dotb(m, m) for m in n0]
    x = [xi + _dotb(xi, pi) for xi, pi in zip(x, p)]
    p = [_dotb(pi, pi) for pi in p]
    x = [xi + _dotb(xi, pi) for xi, pi in zip(x, p)]
    shift = 4
    while (1 << shift) <= L:
        sel = ((ri >> shift) == (ci >> shift)) & ((ri >> (shift - 1)) != (ci >> (shift - 1)))
        xb = [xi.astype(BF16) for xi in x]
        xc = [_dotb(xi, jnp.where(sel, m, 0.0)) for xi, m in zip(xb, ns)]
        x = [xi - _dotb(xci, xbi) for xi, xci, xbi in zip(x, xc, xb)]
        shift += 1
    return x


def _rwkv_kernel(z_ref, mu_ref, kk_ref, ka_ref, rk_ref, w0_ref, wdu_ref, a0_ref, wau_ref,
                 wgu_ref, lnw_ref, lnb_ref, wproj_ref, out_ref, carry_ref, state_ref, y_scr,
                 *, width, heads):
    rows = z_ref.shape[1]
    n = RWKV_HEAD
    L = CHUNK
    nchunks = rows // L

    @pl.when(pl.program_id(1) == 0)
    def _():
        carry_ref[...] = jnp.zeros_like(carry_ref)
        state_ref[...] = jnp.zeros_like(state_ref)

    z = z_ref[0]
    row_id = _iota2((rows, 1), 0)
    zprev = jnp.where(row_id == 0, carry_ref[0:1, :], pltpu.roll(z, 1, axis=0))
    carry_ref[0:1, :] = z[rows - 1:rows, :]
    zm = z + (zprev - z) * mu_ref[...]

    W = width
    r = zm[:, 0:W]
    k = zm[:, W:2 * W]
    v = zm[:, 2 * W:3 * W]
    wd = zm[:, 3 * W:3 * W + DECAY_LORA]
    ad = zm[:, 3 * W + DECAY_LORA:3 * W + DECAY_LORA + AAA_LORA]
    gd = zm[:, 3 * W + DECAY_LORA + AAA_LORA:]

    lw = -DECAY_SCALE * jax.nn.sigmoid(w0_ref[...] + _dotb(jnp.tanh(wd), wdu_ref[...]))
    a = jax.nn.sigmoid(a0_ref[...] + _dotb(ad, wau_ref[...]))
    g = _dotb(jax.nn.sigmoid(gd), wgu_ref[...])

    seg = jnp.where((_iota2((W, W), 0) >> 6) == (_iota2((W, W), 1) >> 6), 1.0, 0.0).astype(BF16)
    segsum = lambda t: _dot_exact(seg, t, 2, m_left=False)
    kk = k * kk_ref[...]
    kk = kk / jnp.maximum(jnp.sqrt(segsum(kk * kk)), 1e-12)
    k_mod = k * (1.0 + (a - 1.0) * ka_ref[...])
    b = kk * a

    ri = _iota2((2 * L, 2 * L), 0)
    ci = _iota2((2 * L, 2 * L), 1)
    rim, cim = ri & (L - 1), ci & (L - 1)
    mask1 = (rim > cim) | ((ri >= L) & (rim == cim))
    tri_bf = jnp.where(_iota2((L, L), 0) >= _iota2((L, L), 1), 1.0, 0.0).astype(BF16)
    eye_n = (_iota2((n, n), 0) == _iota2((n, n), 1)).astype(F32)
    zeros_n = jnp.zeros((L, n), F32)

    units = [(c, h) for c in range(nchunks) for h in range(heads)]
    hsl = lambda h: slice(h * n, (h + 1) * n)
    at_c, rt_c, v_c, glast_c, lhs1_c, rhs1t_c, lhs2t_c = [], [], [], [], [], [], []
    for c in range(nchunks):
        sl = slice(c * L, (c + 1) * L)
        lw_c = lw[sl]
        cum = _dot_exact(tri_bf, lw_c, 3, m_left=True)
        cum_last = cum[L - 1:L, :]
        g_inv = jnp.exp(-cum)
        g_end = jnp.exp(cum_last - cum)
        glast_c.append(jnp.exp(cum_last))
        rt_c.append(r[sl] * jnp.exp(cum))
        at_c.append(kk[sl] * jnp.exp(cum - lw_c))
        v_c.append(v[sl])
        lhs1_c.append(jnp.concatenate([at_c[c], rt_c[c]], axis=0).astype(BF16))
        rhs1t_c.append(jnp.concatenate([b[sl] * g_inv, k_mod[sl] * g_inv], axis=0).T.astype(BF16))
        lhs2t_c.append(jnp.concatenate([b[sl] * g_end, k_mod[sl] * g_end], axis=0).T)
    x1 = [jnp.where(mask1, jnp.dot(lhs1_c[c][:, hsl(h)], rhs1t_c[c][hsl(h), :],
                                   preferred_element_type=F32), 0.0) for c, h in units]
    t_inv = _unit_lower_inverse([m[:L, :L] for m in x1])
    nv = [_dotb(m[:L, L:], v_c[c][:, hsl(h)]) for m, (c, h) in zip(x1, units)]
    pq = [_dotb(t, jnp.concatenate([at_c[c][:, hsl(h)], q], axis=1))
          for t, q, (c, h) in zip(t_inv, nv, units)]
    r2 = []
    for m, pqi, (c, h) in zip(x1, pq, units):
        rhs2 = jnp.concatenate([pqi, jnp.concatenate([zeros_n, -v_c[c][:, hsl(h)]], axis=1)], axis=0)
        lhs2 = jnp.concatenate([m[L:, :], lhs2t_c[c][hsl(h), :]], axis=0)
        r2.append(_dotb(lhs2, rhs2))
    lhs3 = [jnp.concatenate([rt_c[c][:, hsl(h)] - m[:L, :n],
                             eye_n * glast_c[c][:, hsl(h)] - m[L:, :n]], axis=0).astype(BF16)
            for m, (c, h) in zip(r2, units)]

    state = [state_ref[h] for h in range(heads)]
    for c in range(nchunks):
        sl = slice(c * L, (c + 1) * L)
        r3 = [jnp.dot(lhs3[c * heads + h], state[h].astype(BF16), preferred_element_type=F32)
              for h in range(heads)]
        for h in range(heads):
            m = r2[c * heads + h]
            y_scr[sl, hsl(h)] = r3[h][:L] - m[:L, n:]
            state[h] = r3[h][L:] - m[L:, n:]
    for h in range(heads):
        state_ref[h] = state[h]

    y = y_scr[...]
    inv_n = 1.0 / n
    mean = segsum(y) * inv_n
    yc = y - mean
    var = segsum(yc * yc) * inv_n
    yn = yc * lax.rsqrt(var + LNX_EPS) * lnw_ref[...] + lnb_ref[...]
    bonus = segsum(r * k_mod * rk_ref[...]) * v
    out_ref[0] = _dotb((yn + bonus) * g, wproj_ref[...])


def _rwkv_mix(z_rwkv, mu, k_k, k_a, r_k, w0, wdu, a0, wau, wgu, lnw, lnb, wproj, rows=128):
    bsz, seq, ncols = z_rwkv.shape
    W = k_k.shape[1]
    heads = W // RWKV_HEAD
    d = wproj.shape[1]
    full = lambda arr: pl.BlockSpec(arr.shape, lambda b, j: (0,) * arr.ndim)
    params = (mu, k_k, k_a, r_k, w0, wdu, a0, wau, wgu, lnw, lnb, wproj)
    return pl.pallas_call(
        functools.partial(_rwkv_kernel, width=W, heads=heads),
        grid=(bsz, seq // rows),
        in_specs=[pl.BlockSpec((1, rows, ncols), lambda b, j: (b, j, 0))] + [full(p) for p in params],
        out_specs=pl.BlockSpec((1, rows, d), lambda b, j: (b, j, 0)),
        out_shape=jax.ShapeDtypeStruct((bsz, seq, d), F32),
        scratch_shapes=[pltpu.VMEM((8, ncols), F32),
                        pltpu.VMEM((heads, RWKV_HEAD, RWKV_HEAD), F32),
                        pltpu.VMEM((rows, W), F32)],
        compiler_params=pltpu.CompilerParams(dimension_semantics=("parallel", "arbitrary"),
                                             vmem_limit_bytes=VMEM_LIMIT),
        name="rwkv_mix",
    )(z_rwkv, *params)


def _s5_disc_kernel(are_ref, aim_ref, ldt_ref, bre_ref, bim_ref, abr_ref, abi_ref, bbr_ref, bbi_ref):
    dt = jnp.exp(ldt_ref[...])
    are = jnp.minimum(are_ref[...], -1e-4)
    aim = aim_ref[...]
    mag = jnp.exp(dt * are)
    abr = mag * jnp.cos(dt * aim)
    abi = mag * jnp.sin(dt * aim)
    den = are * are + aim * aim
    nr = abr - 1.0
    cre = (nr * are + abi * aim) / den
    cim = (abi * are - nr * aim) / den
    abr_ref[...] = abr
    abi_ref[...] = abi
    br = bre_ref[...]
    bi = bim_ref[...]
    bbr_ref[...] = cre[None] * br - cim[None] * bi
    bbi_ref[...] = cre[None] * bi + cim[None] * br


def _s5_disc(a_re, a_im, log_dt, b_re_c, b_im_c):
    G, P = a_re.shape
    C = b_re_c.shape[0]
    return pl.pallas_call(
        _s5_disc_kernel,
        out_shape=[jax.ShapeDtypeStruct((G, P), F32), jax.ShapeDtypeStruct((G, P), F32),
                   jax.ShapeDtypeStruct((C, G, P), F32), jax.ShapeDtypeStruct((C, G, P), F32)],
        name="s5_disc",
    )(a_re, a_im, log_dt.reshape(G, 1), b_re_c, b_im_c)


def _s5_kernel(u_ref, bmat_ref, cmat_ref, apow_r_ref, apow_i_ref, dskip_ref, wglu_ref, bglu_ref,
               out_ref, st_r_ref, st_i_ref, x_scr, *, nslab):
    rows = u_ref.shape[1]
    ns = apow_r_ref.shape[1]
    sw = ns // nslab
    d = out_ref.shape[2]

    @pl.when(pl.program_id(1) == 0)
    def _():
        st_r_ref[...] = jnp.zeros_like(st_r_ref)
        st_i_ref[...] = jnp.zeros_like(st_i_ref)

    u = u_ref[0]
    sub = _iota2((8, 1), 0)
    ys = []
    for s in range(nslab):
        us = u[:, s * 128:(s + 1) * 128]
        bu = _dotb(us, bmat_ref[s])
        cs = slice(s * sw, (s + 1) * sw)
        ar = apow_r_ref[:, cs]
        ai = apow_i_ref[:, cs]
        a1r, a1i = ar[0:1], ai[0:1]
        a2r, a2i = ar[1:2], ai[1:2]
        a4r, a4i = ar[3:4], ai[3:4]
        cr = st_r_ref[:, cs]
        cim = st_i_ref[:, cs]
        for blk in range(rows // 8):
            xr = bu[blk * 8:(blk + 1) * 8, :sw]
            xi = bu[blk * 8:(blk + 1) * 8, sw:]
            for sh, (pr, pi) in ((1, (a1r, a1i)), (2, (a2r, a2i)), (4, (a4r, a4i))):
                sr = jnp.where(sub >= sh, pltpu.roll(xr, sh, axis=0), 0.0)
                si = jnp.where(sub >= sh, pltpu.roll(xi, sh, axis=0), 0.0)
                xr, xi = xr + pr * sr - pi * si, xi + pr * si + pi * sr
            xr, xi = xr + ar * cr - ai * cim, xi + ar * cim + ai * cr
            cr, cim = xr[7:8], xi[7:8]
            x_scr[blk * 8:(blk + 1) * 8, :sw] = xr
            x_scr[blk * 8:(blk + 1) * 8, sw:] = xi
        st_r_ref[:, cs] = cr
        st_i_ref[:, cs] = cim
        ys.append(_dotb(x_scr[...], cmat_ref[s]))
    y = jnp.concatenate(ys, axis=-1) + dskip_ref[...] * u
    y = 0.5 * y * (1.0 + lax.erf(y * (1.0 / math.sqrt(2.0))))
    zz = _dotb(y, wglu_ref[...]) + bglu_ref[...]
    out_ref[0] = zz[:, :d] * jax.nn.sigmoid(zz[:, d:])


def _s5_mix(u3, bmat, cmat, apow_r, apow_i, dskip, wglu, bglu, rows=128):
    bsz, seq, wc = u3.shape
    nslab = bmat.shape[0]
    ns = apow_r.shape[1]
    d = wglu.shape[1] // 2
    full = lambda arr: pl.BlockSpec(arr.shape, lambda b, j: (0,) * arr.ndim)
    params = (bmat, cmat, apow_r, apow_i, dskip, wglu, bglu)
    return pl.pallas_call(
        functools.partial(_s5_kernel, nslab=nslab),
        grid=(bsz, seq // rows),
        in_specs=[pl.BlockSpec((1, rows, wc), lambda b, j: (b, j, 0))] + [full(p) for p in params],
        out_specs=pl.BlockSpec((1, rows, d), lambda b, j: (b, j, 0)),
        out_shape=jax.ShapeDtypeStruct((bsz, seq, d), F32),
        scratch_shapes=[pltpu.VMEM((1, ns), F32), pltpu.VMEM((1, ns), F32),
                        pltpu.VMEM((rows, 2 * ns // nslab), F32)],
        compiler_params=pltpu.CompilerParams(dimension_semantics=("parallel", "arbitrary"),
                                             vmem_limit_bytes=VMEM_LIMIT),
        name="s5_mix",
    )(u3, *params)


def _s5_pow_kernel(abr_ref, abi_ref, pr_ref, pi_ref):
    ar, ai = abr_ref[...], abi_ref[...]
    cr, ci = ar, ai
    for j in range(8):
        pr_ref[j:j + 1, :] = cr
        pi_ref[j:j + 1, :] = ci
        cr, ci = cr * ar - ci * ai, cr * ai + ci * ar


def _s5_pow(abr_flat, abi_flat):
    ns = abr_flat.shape[1]
    return pl.pallas_call(
        _s5_pow_kernel,
        out_shape=[jax.ShapeDtypeStruct((8, ns), F32), jax.ShapeDtypeStruct((8, ns), F32)],
        name="s5_pow",
    )(abr_flat, abi_flat)


def _merge_ffn_kernel(x_ref, gt_ref, ya_ref, yb_ref, wout_ref, g2_ref, wup_ref, wdn_ref, gf_ref,
                      out_ref, *, final_norm, ff_chunk):
    d = x_ref.shape[1]
    gt = jax.nn.sigmoid(gt_ref[...])
    m = gt[:, :d] * ya_ref[...] + gt[:, d:] * yb_ref[...]
    x1 = x_ref[...] + _dotb(m, wout_ref[...])
    xn = x1 * lax.rsqrt(jnp.mean(x1 * x1, axis=-1, keepdims=True) + NORM_EPS) * g2_ref[...]
    xn = xn.astype(BF16)
    acc = x1
    dff = wup_ref.shape[1]
    for c in range(dff // ff_chunk):
        cs = slice(c * ff_chunk, (c + 1) * ff_chunk)
        h = jnp.dot(xn, wup_ref[:, cs], preferred_element_type=F32)
        h = jnp.square(jnp.maximum(h, 0.0))
        acc = acc + _dotb(h, wdn_ref[cs, :])
    if final_norm:
        acc = acc * lax.rsqrt(jnp.mean(acc * acc, axis=-1, keepdims=True) + NORM_EPS) * gf_ref[...]
    out_ref[...] = acc


def _merge_ffn(x2, gates, ya, yb, wout, g2, wup, wdn, gf, final_norm, tm=256, ff_chunk=1024):
    t, d = x2.shape
    dff = wup.shape[1]
    row = lambda w: pl.BlockSpec((tm, w), lambda i: (i, 0))
    full = lambda arr: pl.BlockSpec(arr.shape, lambda i: (0,) * arr.ndim)
    return pl.pallas_call(
        functools.partial(_merge_ffn_kernel, final_norm=final_norm, ff_chunk=ff_chunk),
        grid=(t // tm,),
        in_specs=[row(d), row(2 * d), row(d), row(d), full(wout), full(g2), full(wup), full(wdn), full(gf)],
        out_specs=row(d),
        out_shape=jax.ShapeDtypeStruct((t, d), F32),
        compiler_params=pltpu.CompilerParams(dimension_semantics=("parallel",),
                                             vmem_limit_bytes=VMEM_LIMIT),
        name="merge_ffn",
    )(x2, gates, ya, yb, wout, g2, wup, wdn, gf)


def _s5_layer_params(a_re, a_im, log_dt, b_re, b_im, c_re, c_im):
    G, P, C = b_re.shape
    gps = 128 // C
    nslab = G // gps
    abr, abi, bbr, bbi = _s5_disc(a_re, a_im, log_dt, jnp.transpose(b_re, (2, 0, 1)),
                                  jnp.transpose(b_im, (2, 0, 1)))
    eye = jnp.eye(gps, dtype=F32)

    def bd_in(bb):
        bb = jnp.transpose(bb, (1, 0, 2)).reshape(nslab, gps, C, P)
        return jnp.einsum('sgcp,gh->sgchp', bb, eye).reshape(nslab, gps * C, gps * P)

    def bd_out(cc):
        cc = cc.reshape(nslab, gps, C, P)
        return jnp.einsum('sgcp,gh->sgphc', cc, eye).reshape(nslab, gps * P, gps * C)

    bmat = jnp.concatenate([bd_in(bbr), bd_in(bbi)], axis=2).astype(BF16)
    cmat = jnp.concatenate([bd_out(c_re), -bd_out(c_im)], axis=1).astype(BF16)
    apow_r, apow_i = _s5_pow(abr.reshape(1, G * P), abi.reshape(1, G * P))
    return bmat, cmat, apow_r, apow_i


def kernel(x, norm1_g, w_in, mu_shift, k_k, k_a, r_k, w0, w_decay_up, a0, w_aaa_up, w_gate_up,
           lnx_w, lnx_b, w_rwkv_proj, a_re, a_im, log_dt, b_re, b_im, c_re, c_im, d_skip, w_glu,
           b_glu, w_out, norm2_g, w_ff_up, w_ff_down, norm_f_g):
    bsz, seq, d = x.shape
    depth = w_in.shape[0]
    W = k_k.shape[1]
    n_shift = mu_shift.shape[1]
    n_ssm = d_skip.shape[1]
    t = bsz * seq
    row = lambda vec: vec.reshape(1, -1)
    x2 = x.reshape(t, d)
    for l in range(depth):
        z_rwkv, u, gates = _in_proj(x2, row(norm1_g[l]), w_in[l].astype(BF16), n_shift, n_ssm)
        y_a = _rwkv_mix(z_rwkv.reshape(bsz, seq, n_shift), row(mu_shift[l]), row(k_k[l]), row(k_a[l]),
                        row(r_k[l]), row(w0[l]), w_decay_up[l].astype(BF16), row(a0[l]),
                        w_aaa_up[l].astype(BF16), w_gate_up[l].astype(BF16), row(lnx_w[l]),
                        row(lnx_b[l]), w_rwkv_proj[l].astype(BF16))
        bmat, cmat, apow_r, apow_i = _s5_layer_params(a_re[l], a_im[l], log_dt[l], b_re[l], b_im[l],
                                                      c_re[l], c_im[l])
        y_b = _s5_mix(u.reshape(bsz, seq, n_ssm), bmat, cmat, apow_r, apow_i, row(d_skip[l]),
                      w_glu[l].astype(BF16), row(b_glu[l]))
        x2 = _merge_ffn(x2, gates, y_a.reshape(t, d), y_b.reshape(t, d), w_out[l].astype(BF16),
                        row(norm2_g[l]), w_ff_up[l].astype(BF16), w_ff_down[l].astype(BF16),
                        row(norm_f_g), final_norm=(l == depth - 1))
    return x2.reshape(bsz, seq, d)
```

```python
import functools
import math

import jax
import jax.numpy as jnp
from jax import lax
from jax.experimental import pallas as pl
from jax.experimental.pallas import tpu as pltpu

F32 = jnp.float32
BF16 = jnp.bfloat16

RWKV_HEAD = 64
DECAY_LORA = 64
AAA_LORA = 64
GATE_LORA = 128
LNX_EPS = 64e-5
DECAY_SCALE = math.exp(-0.5)
SSM_GROUP = 16
SSM_STATE = 64
NORM_EPS = 1e-6

CHUNK = 64
VMEM_LIMIT = 56 * 1024 * 1024


def _dotb(a, b):
    return jnp.dot(a.astype(BF16), b.astype(BF16), preferred_element_type=F32)


def _dotf(a, b):
    return jnp.dot(a, b, preferred_element_type=F32, precision=lax.Precision.HIGHEST)


def _dotf_nt(a, b):
    return lax.dot_general(a, b, (((1,), (1,)), ((), ())), preferred_element_type=F32,
                           precision=lax.Precision.HIGHEST)


def _dotf_tn(a, b):
    return lax.dot_general(a, b, (((0,), (0,)), ((), ())), preferred_element_type=F32,
                           precision=lax.Precision.HIGHEST)


def _iota2(shape, axis):
    return lax.broadcasted_iota(jnp.int32, shape, axis)


def _in_proj_kernel(x_ref, g_ref, w_ref, zr_ref, u_ref, gt_ref, *, n_shift, n_ssm):
    x = x_ref[...]
    xn = x * lax.rsqrt(jnp.mean(x * x, axis=-1, keepdims=True) + NORM_EPS) * g_ref[...]
    z = _dotb(xn, w_ref[...])
    zr_ref[...] = z[:, :n_shift]
    u_ref[...] = z[:, n_shift:n_shift + n_ssm]
    gt_ref[...] = z[:, n_shift + n_ssm:]


def _in_proj(x2, g, w_bf, n_shift, n_ssm, tm=256):
    t, d = x2.shape
    n_in = w_bf.shape[1]
    n_gate = n_in - n_shift - n_ssm
    return pl.pallas_call(
        functools.partial(_in_proj_kernel, n_shift=n_shift, n_ssm=n_ssm),
        grid=(t // tm,),
        in_specs=[pl.BlockSpec((tm, d), lambda i: (i, 0)),
                  pl.BlockSpec((1, d), lambda i: (0, 0)),
                  pl.BlockSpec((d, n_in), lambda i: (0, 0))],
        out_specs=[pl.BlockSpec((tm, n_shift), lambda i: (i, 0)),
                   pl.BlockSpec((tm, n_ssm), lambda i: (i, 0)),
                   pl.BlockSpec((tm, n_gate), lambda i: (i, 0))],
        out_shape=[jax.ShapeDtypeStruct((t, n_shift), F32),
                   jax.ShapeDtypeStruct((t, n_ssm), F32),
                   jax.ShapeDtypeStruct((t, n_gate), F32)],
        compiler_params=pltpu.CompilerParams(dimension_semantics=("parallel",),
                                             vmem_limit_bytes=VMEM_LIMIT),
        name="in_proj",
    )(x2, g, w_bf)


def _dot_exact(m_bf, x, terms, *, m_left):
    acc = None
    rem = x
    for _ in range(terms):
        piece = rem.astype(BF16)
        d = (jnp.dot(m_bf, piece, preferred_element_type=F32) if m_left
             else jnp.dot(piece, m_bf, preferred_element_type=F32))
        acc = d if acc is None else acc + d
        rem = rem - piece.astype(F32)
    return acc


def _segment_sums(ts, n):
    rows, width = ts[0].shape
    tile = 256
    ncol = width // tile
    lg = n.bit_length() - 1
    seg = jnp.where((_iota2((tile, tile), 0) >> lg) == (_iota2((tile, tile), 1) >> lg), 1.0, 0.0).astype(BF16)
    pieces = []
    for t in ts:
        hi = t.astype(BF16)
        lo = (t - hi.astype(F32)).astype(BF16)
        pieces += [part[:, j * tile:(j + 1) * tile] for part in (hi, lo) for j in range(ncol)]
    res = jnp.dot(jnp.concatenate(pieces, axis=0), seg, preferred_element_type=F32)
    blk = lambda i: res[i * rows:(i + 1) * rows]
    return [jnp.concatenate([blk((2 * a) * ncol + j) + blk((2 * a + 1) * ncol + j) for j in range(ncol)], axis=1)
            for a in range(len(ts))]


def _unit_lower_inverse(ns):
    L = ns[0].shape[0]
    ri = _iota2((L, L), 0)
    ci = _iota2((L, L), 1)
    eye = (ri == ci).astype(F32)
    same8 = (ri >> 3) == (ci >> 3)
    n0 = [jnp.where(same8, m, 0.0) for m in ns]
    x = [eye - m for m in n0]
    n0 = [m.astype(BF16) for m in n0]
    p = [_dotb(m, m) for m in n0]
    x = [xi + _dotb(xi, pi) for xi, pi in zip(x, p)]
    p = [_dotb(pi, pi) for pi in p]
    x = [xi + _dotb(xi, pi) for xi, pi in zip(x, p)]
    shift = 4
    while (1 << shift) <= L:
        sel = ((ri >> shift) == (ci >> shift)) & ((ri >> (shift - 1)) != (ci >> (shift - 1)))
        xb = [xi.astype(BF16) for xi in x]
        xc = [_dotb(xi, jnp.where(sel, m, 0.0)) for xi, m in zip(xb, ns)]
        x = [xi - _dotb(xci, xbi) for xi, xci, xbi in zip(x, xc, xb)]
        shift += 1
    return x


def _rwkv_kernel(z_ref, mu_ref, kk_ref, ka_ref, rk_ref, w0_ref, wdu_ref, a0_ref, wau_ref,
                 wgu_ref, lnw_ref, lnb_ref, wproj_ref, out_ref, carry_ref, state_ref, y_scr,
                 *, width, heads):
    rows = z_ref.shape[1]
    n = RWKV_HEAD
    L = CHUNK
    nchunks = rows // L

    @pl.when(pl.program_id(1) == 0)
    def _():
        carry_ref[...] = jnp.zeros_like(carry_ref)
        state_ref[...] = jnp.zeros_like(state_ref)

    z = z_ref[0]
    row_id = _iota2((rows, 1), 0)
    zprev = jnp.where(row_id == 0, carry_ref[0:1, :], pltpu.roll(z, 1, axis=0))
    carry_ref[0:1, :] = z[rows - 1:rows, :]
    zm = z + (zprev - z) * mu_ref[...]

    W = width
    r = zm[:, 0:W]
    k = zm[:, W:2 * W]
    v = zm[:, 2 * W:3 * W]
    wd = zm[:, 3 * W:3 * W + DECAY_LORA]
    ad = zm[:, 3 * W + DECAY_LORA:3 * W + DECAY_LORA + AAA_LORA]
    gd = zm[:, 3 * W + DECAY_LORA + AAA_LORA:]

    lw = -DECAY_SCALE * jax.nn.sigmoid(w0_ref[...] + _dotb(jnp.tanh(wd), wdu_ref[...]))
    a = jax.nn.sigmoid(a0_ref[...] + _dotb(ad, wau_ref[...]))
    g = _dotb(jax.nn.sigmoid(gd), wgu_ref[...])

    kk = k * kk_ref[...]
    k_mod = k * (1.0 + (a - 1.0) * ka_ref[...])
    kk_sq, rk_sum = _segment_sums([kk * kk, r * k_mod * rk_ref[...]], n)
    kk = kk / jnp.maximum(jnp.sqrt(kk_sq), 1e-12)
    b = kk * a

    ri = _iota2((2 * L, 2 * L), 0)
    ci = _iota2((2 * L, 2 * L), 1)
    rim, cim = ri & (L - 1), ci & (L - 1)
    mask1 = (rim > cim) | ((ri >= L) & (rim == cim))
    tri_bf = jnp.where(_iota2((L, L), 0) >= _iota2((L, L), 1), 1.0, 0.0).astype(BF16)
    eye_n = (_iota2((n, n), 0) == _iota2((n, n), 1)).astype(F32)
    zeros_n = jnp.zeros((L, n), F32)

    units = [(c, h) for c in range(nchunks) for h in range(heads)]
    hsl = lambda h: slice(h * n, (h + 1) * n)
    at_c, rt_c, v_c, glast_c, lhs1_c, rhs1t_c, lhs2t_c = [], [], [], [], [], [], []
    for c in range(nchunks):
        sl = slice(c * L, (c + 1) * L)
        lw_c = lw[sl]
        cum = _dot_exact(tri_bf, lw_c, 3, m_left=True)
        cum_last = cum[L - 1:L, :]
        g_inv = jnp.exp(-cum)
        g_end = jnp.exp(cum_last - cum)
        glast_c.append(jnp.exp(cum_last))
        rt_c.append(r[sl] * jnp.exp(cum))
        at_c.append(kk[sl] * jnp.exp(cum - lw_c))
        v_c.append(v[sl])
        lhs1_c.append(jnp.concatenate([at_c[c], rt_c[c]], axis=0).astype(BF16))
        rhs1t_c.append(jnp.concatenate([b[sl] * g_inv, k_mod[sl] * g_inv], axis=0).T.astype(BF16))
        lhs2t_c.append(jnp.concatenate([b[sl] * g_end, k_mod[sl] * g_end], axis=0).T)
    x1 = [jnp.where(mask1, jnp.dot(lhs1_c[c][:, hsl(h)], rhs1t_c[c][hsl(h), :],
                                   preferred_element_type=F32), 0.0) for c, h in units]
    t_inv = _unit_lower_inverse([m[:L, :L] for m in x1])
    nv = [_dotb(m[:L, L:], v_c[c][:, hsl(h)]) for m, (c, h) in zip(x1, units)]
    pq = [_dotb(t, jnp.concatenate([at_c[c][:, hsl(h)], q], axis=1))
          for t, q, (c, h) in zip(t_inv, nv, units)]
    r2 = []
    for m, pqi, (c, h) in zip(x1, pq, units):
        rhs2 = jnp.concatenate([pqi, jnp.concatenate([zeros_n, -v_c[c][:, hsl(h)]], axis=1)], axis=0)
        lhs2 = jnp.concatenate([m[L:, :], lhs2t_c[c][hsl(h), :]], axis=0)
        r2.append(_dotb(lhs2, rhs2))
    lhs3 = [jnp.concatenate([rt_c[c][:, hsl(h)] - m[:L, :n],
                             eye_n * glast_c[c][:, hsl(h)] - m[L:, :n]], axis=0).astype(BF16)
            for m, (c, h) in zip(r2, units)]

    state = [state_ref[h] for h in range(heads)]
    for c in range(nchunks):
        sl = slice(c * L, (c + 1) * L)
        r3 = [jnp.dot(lhs3[c * heads + h], state[h].astype(BF16), preferred_element_type=F32)
              for h in range(heads)]
        for h in range(heads):
            m = r2[c * heads + h]
            y_scr[sl, hsl(h)] = r3[h][:L] - m[:L, n:]
            state[h] = r3[h][L:] - m[L:, n:]
    for h in range(heads):
        state_ref[h] = state[h]

    y = y_scr[...]
    inv_n = 1.0 / n
    yc = y - _segment_sums([y], n)[0] * inv_n
    var = _segment_sums([yc * yc], n)[0] * inv_n
    yn = yc * lax.rsqrt(var + LNX_EPS) * lnw_ref[...] + lnb_ref[...]
    out_ref[0] = _dotb((yn + rk_sum * v) * g, wproj_ref[...])


def _rwkv_mix(z_rwkv, mu, k_k, k_a, r_k, w0, wdu, a0, wau, wgu, lnw, lnb, wproj, rows=256):
    bsz, seq, ncols = z_rwkv.shape
    W = k_k.shape[1]
    heads = W // RWKV_HEAD
    d = wproj.shape[1]
    full = lambda arr: pl.BlockSpec(arr.shape, lambda b, j: (0,) * arr.ndim)
    params = (mu, k_k, k_a, r_k, w0, wdu, a0, wau, wgu, lnw, lnb, wproj)
    return pl.pallas_call(
        functools.partial(_rwkv_kernel, width=W, heads=heads),
        grid=(bsz, seq // rows),
        in_specs=[pl.BlockSpec((1, rows, ncols), lambda b, j: (b, j, 0))] + [full(p) for p in params],
        out_specs=pl.BlockSpec((1, rows, d), lambda b, j: (b, j, 0)),
        out_shape=jax.ShapeDtypeStruct((bsz, seq, d), F32),
        scratch_shapes=[pltpu.VMEM((8, ncols), F32),
                        pltpu.VMEM((heads, RWKV_HEAD, RWKV_HEAD), F32),
                        pltpu.VMEM((rows, W), F32)],
        compiler_params=pltpu.CompilerParams(dimension_semantics=("parallel", "arbitrary"),
                                             vmem_limit_bytes=VMEM_LIMIT),
        name="rwkv_mix",
    )(z_rwkv, *params)


def _s5_disc_kernel(are_ref, aim_ref, ldt_ref, bre_ref, bim_ref, abr_ref, abi_ref, bbr_ref, bbi_ref):
    dt = jnp.exp(ldt_ref[...])
    are = jnp.minimum(are_ref[...], -1e-4)
    aim = aim_ref[...]
    mag = jnp.exp(dt * are)
    abr = mag * jnp.cos(dt * aim)
    abi = mag * jnp.sin(dt * aim)
    den = are * are + aim * aim
    nr = abr - 1.0
    cre = (nr * are + abi * aim) / den
    cim = (abi * are - nr * aim) / den
    abr_ref[...] = abr
    abi_ref[...] = abi
    br = bre_ref[...]
    bi = bim_ref[...]
    bbr_ref[...] = cre[None] * br - cim[None] * bi
    bbi_ref[...] = cre[None] * bi + cim[None] * br


def _s5_disc(a_re, a_im, log_dt, b_re_c, b_im_c):
    G, P = a_re.shape
    C = b_re_c.shape[0]
    return pl.pallas_call(
        _s5_disc_kernel,
        out_shape=[jax.ShapeDtypeStruct((G, P), F32), jax.ShapeDtypeStruct((G, P), F32),
                   jax.ShapeDtypeStruct((C, G, P), F32), jax.ShapeDtypeStruct((C, G, P), F32)],
        name="s5_disc",
    )(a_re, a_im, log_dt.reshape(G, 1), b_re_c, b_im_c)


def _s5_kernel(u_ref, bmat_ref, cmat_ref, abr_ref, abi_ref, dskip_ref, wglu_ref, bglu_ref,
               out_ref, st_r_ref, st_i_ref, x_scr, *, nslab):
    nb, rt, wc = u_ref.shape
    rows = nb * rt
    ns = abr_ref.shape[1]
    sw = ns // nslab
    d = out_ref.shape[2]
    lg_nb, lg_rt = nb.bit_length() - 1, rt.bit_length() - 1

    @pl.when(pl.program_id(1) == 0)
    def _():
        st_r_ref[...] = jnp.zeros_like(st_r_ref)
        st_i_ref[...] = jnp.zeros_like(st_i_ref)

    u = u_ref[...].reshape(rows, wc)
    ri = _iota2((rows, rows), 0)
    ci = _iota2((rows, rows), 1)
    to_tb = jnp.where(ci == ((ri & (nb - 1)) << lg_rt) + (ri >> lg_nb), 1.0, 0.0).astype(BF16)
    to_bt = jnp.where(ci == ((ri & (rt - 1)) << lg_nb) + (ri >> lg_rt), 1.0, 0.0).astype(BF16)
    u_tb = jnp.dot(to_tb, u.astype(BF16), preferred_element_type=F32).astype(BF16)
    ys = []
    for s in range(nslab):
        bu = jnp.dot(u_tb[:, s * 128:(s + 1) * 128], bmat_ref[s], preferred_element_type=F32)
        cs = slice(s * sw, (s + 1) * sw)
        ar = abr_ref[:, cs]
        ai = abi_ref[:, cs]
        xr = st_r_ref[:, cs]
        xi = st_i_ref[:, cs]
        for t in range(rt):
            ts = slice(t * nb, (t + 1) * nb)
            xr, xi = ar * xr - ai * xi + bu[ts, :sw], ar * xi + ai * xr + bu[ts, sw:]
            x_scr[ts, :sw] = xr
            x_scr[ts, sw:] = xi
        st_r_ref[:, cs] = xr
        st_i_ref[:, cs] = xi
        ys.append(_dotb(x_scr[...], cmat_ref[s]))
    y = _dot_exact(to_bt, jnp.concatenate(ys, axis=-1), 2, m_left=True) + dskip_ref[...] * u
    y = 0.5 * y * (1.0 + lax.erf(y * (1.0 / math.sqrt(2.0))))
    zz = _dotb(y, wglu_ref[...]) + bglu_ref[...]
    out_ref[...] = (zz[:, :d] * jax.nn.sigmoid(zz[:, d:])).reshape(nb, rt, d)


def _s5_mix(u3, bmat, cmat, abr, abi, dskip, wglu, bglu, nb=8, rt=32):
    bsz, seq, wc = u3.shape
    assert nb & (nb - 1) == 0 and rt & (rt - 1) == 0 and rt % 8 == 0
    nslab = bmat.shape[0]
    ns = abr.shape[1]
    d = wglu.shape[1] // 2
    full = lambda arr: pl.BlockSpec(arr.shape, lambda b, j: (0,) * arr.ndim)
    params = (bmat, cmat, abr, abi, dskip, wglu, bglu)
    return pl.pallas_call(
        functools.partial(_s5_kernel, nslab=nslab),
        grid=(bsz // nb, seq // rt),
        in_specs=[pl.BlockSpec((nb, rt, wc), lambda b, j: (b, j, 0))] + [full(p) for p in params],
        out_specs=pl.BlockSpec((nb, rt, d), lambda b, j: (b, j, 0)),
        out_shape=jax.ShapeDtypeStruct((bsz, seq, d), F32),
        scratch_shapes=[pltpu.VMEM((nb, ns), F32), pltpu.VMEM((nb, ns), F32),
                        pltpu.VMEM((nb * rt, 2 * ns // nslab), F32)],
        compiler_params=pltpu.CompilerParams(dimension_semantics=("parallel", "arbitrary"),
                                             vmem_limit_bytes=VMEM_LIMIT),
        name="s5_mix",
    )(u3, *params)


def _merge_ffn_kernel(x_ref, gt_ref, ya_ref, yb_ref, wout_ref, g2_ref, wup_ref, wdn_ref, gf_ref,
                      out_ref, *, final_norm, ff_chunk):
    d = x_ref.shape[1]
    gt = jax.nn.sigmoid(gt_ref[...])
    m = gt[:, :d] * ya_ref[...] + gt[:, d:] * yb_ref[...]
    x1 = x_ref[...] + _dotb(m, wout_ref[...])
    xn = x1 * lax.rsqrt(jnp.mean(x1 * x1, axis=-1, keepdims=True) + NORM_EPS) * g2_ref[...]
    xn = xn.astype(BF16)
    acc = x1
    dff = wup_ref.shape[1]
    for c in range(dff // ff_chunk):
        cs = slice(c * ff_chunk, (c + 1) * ff_chunk)
        h = jnp.dot(xn, wup_ref[:, cs], preferred_element_type=F32)
        h = jnp.square(jnp.maximum(h, 0.0))
        acc = acc + _dotb(h, wdn_ref[cs, :])
    if final_norm:
        acc = acc * lax.rsqrt(jnp.mean(acc * acc, axis=-1, keepdims=True) + NORM_EPS) * gf_ref[...]
    out_ref[...] = acc


def _merge_ffn(x2, gates, ya, yb, wout, g2, wup, wdn, gf, final_norm, tm=256, ff_chunk=1024):
    t, d = x2.shape
    dff = wup.shape[1]
    row = lambda w: pl.BlockSpec((tm, w), lambda i: (i, 0))
    full = lambda arr: pl.BlockSpec(arr.shape, lambda i: (0,) * arr.ndim)
    return pl.pallas_call(
        functools.partial(_merge_ffn_kernel, final_norm=final_norm, ff_chunk=ff_chunk),
        grid=(t // tm,),
        in_specs=[row(d), row(2 * d), row(d), row(d), full(wout), full(g2), full(wup), full(wdn), full(gf)],
        out_specs=row(d),
        out_shape=jax.ShapeDtypeStruct((t, d), F32),
        compiler_params=pltpu.CompilerParams(dimension_semantics=("parallel",),
                                             vmem_limit_bytes=VMEM_LIMIT),
        name="merge_ffn",
    )(x2, gates, ya, yb, wout, g2, wup, wdn, gf)


def _s5_layer_params(a_re, a_im, log_dt, b_re, b_im, c_re, c_im):
    G, P, C = b_re.shape
    gps = 128 // C
    nslab = G // gps
    abr, abi, bbr, bbi = _s5_disc(a_re, a_im, log_dt, jnp.transpose(b_re, (2, 0, 1)),
                                  jnp.transpose(b_im, (2, 0, 1)))
    eye = jnp.eye(gps, dtype=F32)

    def bd_in(bb):
        bb = jnp.transpose(bb, (1, 0, 2)).reshape(nslab, gps, C, P)
        return jnp.einsum('sgcp,gh->sgchp', bb, eye).reshape(nslab, gps * C, gps * P)

    def bd_out(cc):
        cc = cc.reshape(nslab, gps, C, P)
        return jnp.einsum('sgcp,gh->sgphc', cc, eye).reshape(nslab, gps * P, gps * C)

    bmat = jnp.concatenate([bd_in(bbr), bd_in(bbi)], axis=2).astype(BF16)
    cmat = jnp.concatenate([bd_out(c_re), -bd_out(c_im)], axis=1).astype(BF16)
    return bmat, cmat, abr.reshape(1, G * P), abi.reshape(1, G * P)


def kernel(x, norm1_g, w_in, mu_shift, k_k, k_a, r_k, w0, w_decay_up, a0, w_aaa_up, w_gate_up,
           lnx_w, lnx_b, w_rwkv_proj, a_re, a_im, log_dt, b_re, b_im, c_re, c_im, d_skip, w_glu,
           b_glu, w_out, norm2_g, w_ff_up, w_ff_down, norm_f_g):
    bsz, seq, d = x.shape
    depth = w_in.shape[0]
    W = k_k.shape[1]
    n_shift = mu_shift.shape[1]
    n_ssm = d_skip.shape[1]
    t = bsz * seq
    row = lambda vec: vec.reshape(1, -1)
    x2 = x.reshape(t, d)
    for l in range(depth):
        z_rwkv, u, gates = _in_proj(x2, row(norm1_g[l]), w_in[l].astype(BF16), n_shift, n_ssm)
        y_a = _rwkv_mix(z_rwkv.reshape(bsz, seq, n_shift), row(mu_shift[l]), row(k_k[l]), row(k_a[l]),
                        row(r_k[l]), row(w0[l]), w_decay_up[l].astype(BF16), row(a0[l]),
                        w_aaa_up[l].astype(BF16), w_gate_up[l].astype(BF16), row(lnx_w[l]),
                        row(lnx_b[l]), w_rwkv_proj[l].astype(BF16))
        bmat, cmat, abr, abi = _s5_layer_params(a_re[l], a_im[l], log_dt[l], b_re[l], b_im[l],
                                                      c_re[l], c_im[l])
        y_b = _s5_mix(u.reshape(bsz, seq, n_ssm), bmat, cmat, abr, abi, row(d_skip[l]),
                      w_glu[l].astype(BF16), row(b_glu[l]))
        x2 = _merge_ffn(x2, gates, y_a.reshape(t, d), y_b.reshape(t, d), w_out[l].astype(BF16),
                        row(norm2_g[l]), w_ff_up[l].astype(BF16), w_ff_down[l].astype(BF16),
                        row(norm_f_g), final_norm=(l == depth - 1))
    return x2.reshape(bsz, seq, d)
```

```python
import functools
import math

import jax
import jax.numpy as jnp
from jax import lax
from jax.experimental import pallas as pl
from jax.experimental.pallas import tpu as pltpu

F32 = jnp.float32
BF16 = jnp.bfloat16

RWKV_HEAD = 64
DECAY_LORA = 64
AAA_LORA = 64
LNX_EPS = 64e-5
DECAY_SCALE = math.exp(-0.5)
NORM_EPS = 1e-6

CHUNK = 64
VMEM_LIMIT = 56 * 1024 * 1024


def _dotb(a, b):
    return jnp.dot(a.astype(BF16), b.astype(BF16), preferred_element_type=F32)


def _iota2(shape, axis):
    return lax.broadcasted_iota(jnp.int32, shape, axis)


def _layer_spec(arr, layer):
    return pl.BlockSpec((None,) + arr.shape[1:], lambda *_: (layer,) + (0,) * (arr.ndim - 1),
                        pipeline_mode=pl.Buffered(1))


def _in_proj_kernel(x_ref, g_ref, w_ref, zr_ref, u_ref, gt_ref, *, n_shift, n_ssm):
    x = x_ref[...]
    xn = x * lax.rsqrt(jnp.mean(x * x, axis=-1, keepdims=True) + NORM_EPS) * g_ref[...]
    z = _dotb(xn, w_ref[...])
    zr_ref[...] = z[:, :n_shift]
    u_ref[...] = z[:, n_shift:n_shift + n_ssm]
    gt_ref[...] = z[:, n_shift + n_ssm:]


def _in_proj(x2, g, w_bf, layer, n_shift, n_ssm, tm=512):
    t, d = x2.shape
    n_in = w_bf.shape[-1]
    n_gate = n_in - n_shift - n_ssm
    return pl.pallas_call(
        functools.partial(_in_proj_kernel, n_shift=n_shift, n_ssm=n_ssm),
        grid=(t // tm,),
        in_specs=[pl.BlockSpec((tm, d), lambda i: (i, 0)),
                  _layer_spec(g, layer), _layer_spec(w_bf, layer)],
        out_specs=[pl.BlockSpec((tm, n_shift), lambda i: (i, 0)),
                   pl.BlockSpec((tm, n_ssm), lambda i: (i, 0)),
                   pl.BlockSpec((tm, n_gate), lambda i: (i, 0))],
        out_shape=[jax.ShapeDtypeStruct((t, n_shift), F32),
                   jax.ShapeDtypeStruct((t, n_ssm), F32),
                   jax.ShapeDtypeStruct((t, n_gate), F32)],
        compiler_params=pltpu.CompilerParams(dimension_semantics=("parallel",),
                                             vmem_limit_bytes=VMEM_LIMIT),
        name="in_proj",
    )(x2, g, w_bf)


def _dot_exact(m_bf, x, terms, *, m_left):
    acc = None
    rem = x
    for _ in range(terms):
        piece = rem.astype(BF16)
        d = (jnp.dot(m_bf, piece, preferred_element_type=F32) if m_left
             else jnp.dot(piece, m_bf, preferred_element_type=F32))
        acc = d if acc is None else acc + d
        rem = rem - piece.astype(F32)
    return acc


def _segment_sums(ts, n):
    rows, width = ts[0].shape
    tile = 256
    ncol = width // tile
    lg = n.bit_length() - 1
    seg = jnp.where((_iota2((tile, tile), 0) >> lg) == (_iota2((tile, tile), 1) >> lg), 1.0, 0.0).astype(BF16)
    pieces = []
    for t in ts:
        hi = t.astype(BF16)
        lo = (t - hi.astype(F32)).astype(BF16)
        pieces += [part[:, j * tile:(j + 1) * tile] for part in (hi, lo) for j in range(ncol)]
    res = jnp.dot(jnp.concatenate(pieces, axis=0), seg, preferred_element_type=F32)
    blk = lambda i: res[i * rows:(i + 1) * rows]
    return [jnp.concatenate([blk((2 * a) * ncol + j) + blk((2 * a + 1) * ncol + j) for j in range(ncol)], axis=1)
            for a in range(len(ts))]


def _unit_lower_inverse(ns):
    L = ns[0].shape[0]
    ri = _iota2((L, L), 0)
    ci = _iota2((L, L), 1)
    eye = (ri == ci).astype(F32)
    same8 = (ri >> 3) == (ci >> 3)
    n0 = [jnp.where(same8, m, 0.0) for m in ns]
    x = [eye - m for m in n0]
    n0 = [m.astype(BF16) for m in n0]
    p = [_dotb(m, m) for m in n0]
    x = [xi + _dotb(xi, pi) for xi, pi in zip(x, p)]
    p = [_dotb(pi, pi) for pi in p]
    x = [xi + _dotb(xi, pi) for xi, pi in zip(x, p)]
    shift = 4
    while (1 << shift) <= L:
        sel = ((ri >> shift) == (ci >> shift)) & ((ri >> (shift - 1)) != (ci >> (shift - 1)))
        xb = [xi.astype(BF16) for xi in x]
        xc = [_dotb(xi, jnp.where(sel, m, 0.0)) for xi, m in zip(xb, ns)]
        x = [xi - _dotb(xci, xbi) for xi, xci, xbi in zip(x, xc, xb)]
        shift += 1
    return x


def _rwkv_kernel(z_ref, mu_ref, kk_ref, ka_ref, rk_ref, w0_ref, wdu_ref, a0_ref, wau_ref,
                 wgu_ref, lnw_ref, lnb_ref, wproj_ref, out_ref, carry_ref, state_ref, y_scr,
                 *, width, heads):
    rows = z_ref.shape[1]
    n = RWKV_HEAD
    L = CHUNK
    nchunks = rows // L

    @pl.when(pl.program_id(1) == 0)
    def _():
        carry_ref[...] = jnp.zeros_like(carry_ref)
        state_ref[...] = jnp.zeros_like(state_ref)

    z = z_ref[0]
    row_id = _iota2((rows, 1), 0)
    zprev = jnp.where(row_id == 0, carry_ref[0:1, :], pltpu.roll(z, 1, axis=0))
    carry_ref[0:1, :] = z[rows - 1:rows, :]
    zm = z + (zprev - z) * mu_ref[...]

    W = width
    r = zm[:, 0:W]
    k = zm[:, W:2 * W]
    v = zm[:, 2 * W:3 * W]
    wd = zm[:, 3 * W:3 * W + DECAY_LORA]
    ad = zm[:, 3 * W + DECAY_LORA:3 * W + DECAY_LORA + AAA_LORA]
    gd = zm[:, 3 * W + DECAY_LORA + AAA_LORA:]

    lw = -DECAY_SCALE * jax.nn.sigmoid(w0_ref[...] + _dotb(jnp.tanh(wd), wdu_ref[...]))
    a = jax.nn.sigmoid(a0_ref[...] + _dotb(ad, wau_ref[...]))
    g = _dotb(jax.nn.sigmoid(gd), wgu_ref[...])

    kk = k * kk_ref[...]
    k_mod = k * (1.0 + (a - 1.0) * ka_ref[...])
    kk_sq, rk_sum = _segment_sums([kk * kk, r * k_mod * rk_ref[...]], n)
    kk = kk / jnp.maximum(jnp.sqrt(kk_sq), 1e-12)
    b = kk * a

    ri = _iota2((2 * L, 2 * L), 0)
    ci = _iota2((2 * L, 2 * L), 1)
    rim, cim = ri & (L - 1), ci & (L - 1)
    mask1 = (rim > cim) | ((ri >= L) & (rim == cim))
    tri_bf = jnp.where(_iota2((L, L), 0) >= _iota2((L, L), 1), 1.0, 0.0).astype(BF16)
    eye_n = (_iota2((n, n), 0) == _iota2((n, n), 1)).astype(F32)
    zeros_n = jnp.zeros((L, n), F32)

    units = [(c, h) for c in range(nchunks) for h in range(heads)]
    hsl = lambda h: slice(h * n, (h + 1) * n)
    at_c, rt_c, v_c, glast_c, lhs1_c, rhs1t_c, lhs2t_c = [], [], [], [], [], [], []
    for c in range(nchunks):
        sl = slice(c * L, (c + 1) * L)
        lw_c = lw[sl]
        cum = _dot_exact(tri_bf, lw_c, 3, m_left=True)
        cum_last = cum[L - 1:L, :]
        g_inv = jnp.exp(-cum)
        g_end = jnp.exp(cum_last - cum)
        glast_c.append(jnp.exp(cum_last))
        rt_c.append(r[sl] * jnp.exp(cum))
        at_c.append(kk[sl] * jnp.exp(cum - lw_c))
        v_c.append(v[sl])
        lhs1_c.append(jnp.concatenate([at_c[c], rt_c[c]], axis=0).astype(BF16))
        rhs1t_c.append(jnp.concatenate([b[sl] * g_inv, k_mod[sl] * g_inv], axis=0).T.astype(BF16))
        lhs2t_c.append(jnp.concatenate([b[sl] * g_end, k_mod[sl] * g_end], axis=0).T)
    x1 = [jnp.where(mask1, jnp.dot(lhs1_c[c][:, hsl(h)], rhs1t_c[c][hsl(h), :],
                                   preferred_element_type=F32), 0.0) for c, h in units]
    t_inv = _unit_lower_inverse([m[:L, :L] for m in x1])
    nv = [_dotb(m[:L, L:], v_c[c][:, hsl(h)]) for m, (c, h) in zip(x1, units)]
    pq = [_dotb(t, jnp.concatenate([at_c[c][:, hsl(h)], q], axis=1))
          for t, q, (c, h) in zip(t_inv, nv, units)]
    r2 = []
    for m, pqi, (c, h) in zip(x1, pq, units):
        rhs2 = jnp.concatenate([pqi, jnp.concatenate([zeros_n, -v_c[c][:, hsl(h)]], axis=1)], axis=0)
        lhs2 = jnp.concatenate([m[L:, :], lhs2t_c[c][hsl(h), :]], axis=0)
        r2.append(_dotb(lhs2, rhs2))
    lhs3 = [jnp.concatenate([rt_c[c][:, hsl(h)] - m[:L, :n],
                             eye_n * glast_c[c][:, hsl(h)] - m[L:, :n]], axis=0).astype(BF16)
            for m, (c, h) in zip(r2, units)]

    state = [state_ref[h] for h in range(heads)]
    for c in range(nchunks):
        sl = slice(c * L, (c + 1) * L)
        r3 = [jnp.dot(lhs3[c * heads + h], state[h].astype(BF16), preferred_element_type=F32)
              for h in range(heads)]
        for h in range(heads):
            m = r2[c * heads + h]
            y_scr[sl, hsl(h)] = r3[h][:L] - m[:L, n:]
            state[h] = r3[h][L:] - m[L:, n:]
    for h in range(heads):
        state_ref[h] = state[h]

    y = y_scr[...]
    inv_n = 1.0 / n
    yc = y - _segment_sums([y], n)[0] * inv_n
    var = _segment_sums([yc * yc], n)[0] * inv_n
    yn = yc * lax.rsqrt(var + LNX_EPS) * lnw_ref[...] + lnb_ref[...]
    out_ref[0] = _dotb((yn + rk_sum * v) * g, wproj_ref[...])


def _rwkv_mix(z_rwkv, layer, mu, k_k, k_a, r_k, w0, wdu, a0, wau, wgu, lnw, lnb, wproj, rows=256):
    bsz, seq, ncols = z_rwkv.shape
    W = k_k.shape[-1]
    heads = W // RWKV_HEAD
    d = wproj.shape[-1]
    full = lambda arr: _layer_spec(arr, layer)
    params = (mu, k_k, k_a, r_k, w0, wdu, a0, wau, wgu, lnw, lnb, wproj)
    return pl.pallas_call(
        functools.partial(_rwkv_kernel, width=W, heads=heads),
        grid=(bsz, seq // rows),
        in_specs=[pl.BlockSpec((1, rows, ncols), lambda b, j: (b, j, 0))] + [full(p) for p in params],
        out_specs=pl.BlockSpec((1, rows, d), lambda b, j: (b, j, 0)),
        out_shape=jax.ShapeDtypeStruct((bsz, seq, d), F32),
        scratch_shapes=[pltpu.VMEM((8, ncols), F32),
                        pltpu.VMEM((heads, RWKV_HEAD, RWKV_HEAD), F32),
                        pltpu.VMEM((rows, W), F32)],
        compiler_params=pltpu.CompilerParams(dimension_semantics=("parallel", "arbitrary"),
                                             vmem_limit_bytes=VMEM_LIMIT),
        name="rwkv_mix",
    )(z_rwkv, *params)


def _s5_disc_kernel(are_ref, aim_ref, ldt_ref, bre_ref, bim_ref, abr_ref, abi_ref, bbr_ref, bbi_ref):
    dt = jnp.exp(ldt_ref[...])
    are = jnp.minimum(are_ref[...], -1e-4)
    aim = aim_ref[...]
    mag = jnp.exp(dt * are)
    abr = mag * jnp.cos(dt * aim)
    abi = mag * jnp.sin(dt * aim)
    den = are * are + aim * aim
    nr = abr - 1.0
    cre = (nr * are + abi * aim) / den
    cim = (abi * are - nr * aim) / den
    abr_ref[...] = abr
    abi_ref[...] = abi
    br = bre_ref[...]
    bi = bim_ref[...]
    bbr_ref[...] = cre[None] * br - cim[None] * bi
    bbi_ref[...] = cre[None] * bi + cim[None] * br


def _s5_disc(a_re, a_im, log_dt, b_re_c, b_im_c):
    G, P = a_re.shape
    C = b_re_c.shape[0]
    return pl.pallas_call(
        _s5_disc_kernel,
        out_shape=[jax.ShapeDtypeStruct((G, P), F32), jax.ShapeDtypeStruct((G, P), F32),
                   jax.ShapeDtypeStruct((C, G, P), F32), jax.ShapeDtypeStruct((C, G, P), F32)],
        name="s5_disc",
    )(a_re, a_im, log_dt.reshape(G, 1), b_re_c, b_im_c)


def _s5_kernel(u_ref, bmat_ref, cmat_ref, abr_ref, abi_ref, dskip_ref, wglu_ref, bglu_ref,
               out_ref, st_r_ref, st_i_ref, x_scr, *, nslab):
    nb, rt, wc = u_ref.shape
    rows = nb * rt
    ns = abr_ref.shape[1]
    sw = ns // nslab
    d = out_ref.shape[2]
    lg_nb, lg_rt = nb.bit_length() - 1, rt.bit_length() - 1

    @pl.when(pl.program_id(1) == 0)
    def _():
        st_r_ref[...] = jnp.zeros_like(st_r_ref)
        st_i_ref[...] = jnp.zeros_like(st_i_ref)

    u = u_ref[...].reshape(rows, wc)
    ri = _iota2((rows, rows), 0)
    ci = _iota2((rows, rows), 1)
    to_tb = jnp.where(ci == ((ri & (nb - 1)) << lg_rt) + (ri >> lg_nb), 1.0, 0.0).astype(BF16)
    to_bt = jnp.where(ci == ((ri & (rt - 1)) << lg_nb) + (ri >> lg_rt), 1.0, 0.0).astype(BF16)
    u_tb = jnp.dot(to_tb, u.astype(BF16), preferred_element_type=F32).astype(BF16)
    ys = []
    for s in range(nslab):
        bu = jnp.dot(u_tb[:, s * 128:(s + 1) * 128], bmat_ref[s], preferred_element_type=F32)
        cs = slice(s * sw, (s + 1) * sw)
        ar = abr_ref[:, cs]
        ai = abi_ref[:, cs]
        xr = st_r_ref[:, cs]
        xi = st_i_ref[:, cs]
        for t in range(rt):
            ts = slice(t * nb, (t + 1) * nb)
            xr, xi = ar * xr - ai * xi + bu[ts, :sw], ar * xi + ai * xr + bu[ts, sw:]
            x_scr[ts, :sw] = xr
            x_scr[ts, sw:] = xi
        st_r_ref[:, cs] = xr
        st_i_ref[:, cs] = xi
        ys.append(_dotb(x_scr[...], cmat_ref[s]))
    y = _dot_exact(to_bt, jnp.concatenate(ys, axis=-1), 2, m_left=True) + dskip_ref[...] * u
    y = 0.5 * y * (1.0 + lax.erf(y * (1.0 / math.sqrt(2.0))))
    zz = _dotb(y, wglu_ref[...]) + bglu_ref[...]
    out_ref[...] = (zz[:, :d] * jax.nn.sigmoid(zz[:, d:])).reshape(nb, rt, d)


def _s5_mix(u3, layer, bmat, cmat, abr, abi, dskip, wglu, bglu, nb=8, rt=64):
    bsz, seq, wc = u3.shape
    assert nb & (nb - 1) == 0 and rt & (rt - 1) == 0 and rt % 8 == 0
    nslab = bmat.shape[1]
    ns = abr.shape[-1]
    d = wglu.shape[-1] // 2
    full = lambda arr: _layer_spec(arr, layer)
    params = (bmat, cmat, abr, abi, dskip, wglu, bglu)
    return pl.pallas_call(
        functools.partial(_s5_kernel, nslab=nslab),
        grid=(bsz // nb, seq // rt),
        in_specs=[pl.BlockSpec((nb, rt, wc), lambda b, j: (b, j, 0))] + [full(p) for p in params],
        out_specs=pl.BlockSpec((nb, rt, d), lambda b, j: (b, j, 0)),
        out_shape=jax.ShapeDtypeStruct((bsz, seq, d), F32),
        scratch_shapes=[pltpu.VMEM((nb, ns), F32), pltpu.VMEM((nb, ns), F32),
                        pltpu.VMEM((nb * rt, 2 * ns // nslab), F32)],
        compiler_params=pltpu.CompilerParams(dimension_semantics=("parallel", "arbitrary"),
                                             vmem_limit_bytes=VMEM_LIMIT),
        name="s5_mix",
    )(u3, *params)


def _merge_ffn_kernel(x_ref, gt_ref, ya_ref, yb_ref, wout_ref, g2_ref, wup_ref, wdn_ref, gf_ref,
                      out_ref, *, final_norm, ff_chunk):
    d = x_ref.shape[1]
    gt = jax.nn.sigmoid(gt_ref[...])
    m = gt[:, :d] * ya_ref[...] + gt[:, d:] * yb_ref[...]
    x1 = x_ref[...] + _dotb(m, wout_ref[...])
    xn = x1 * lax.rsqrt(jnp.mean(x1 * x1, axis=-1, keepdims=True) + NORM_EPS) * g2_ref[...]
    xn = xn.astype(BF16)
    acc = x1
    dff = wup_ref.shape[1]
    for c in range(dff // ff_chunk):
        cs = slice(c * ff_chunk, (c + 1) * ff_chunk)
        h = jnp.dot(xn, wup_ref[:, cs], preferred_element_type=F32)
        h = jnp.square(jnp.maximum(h, 0.0))
        acc = acc + _dotb(h, wdn_ref[cs, :])
    if final_norm:
        acc = acc * lax.rsqrt(jnp.mean(acc * acc, axis=-1, keepdims=True) + NORM_EPS) * gf_ref[...]
    out_ref[...] = acc


def _merge_ffn(x2, gates, ya, yb, layer, wout, g2, wup, wdn, gf, final_norm, tm=512, ff_chunk=1024):
    t, d = x2.shape
    row = lambda w: pl.BlockSpec((tm, w), lambda i: (i, 0))
    full = lambda arr: _layer_spec(arr, layer)
    return pl.pallas_call(
        functools.partial(_merge_ffn_kernel, final_norm=final_norm, ff_chunk=ff_chunk),
        grid=(t // tm,),
        in_specs=[row(d), row(2 * d), row(d), row(d), full(wout), full(g2), full(wup), full(wdn),
                  pl.BlockSpec(gf.shape, lambda i: (0, 0))],
        out_specs=row(d),
        out_shape=jax.ShapeDtypeStruct((t, d), F32),
        compiler_params=pltpu.CompilerParams(dimension_semantics=("parallel",),
                                             vmem_limit_bytes=VMEM_LIMIT),
        name="merge_ffn",
    )(x2, gates, ya, yb, wout, g2, wup, wdn, gf)


def _s5_layer_params(a_re, a_im, log_dt, b_re, b_im, c_re, c_im):
    G, P, C = b_re.shape
    gps = 128 // C
    nslab = G // gps
    abr, abi, bbr, bbi = _s5_disc(a_re, a_im, log_dt, jnp.transpose(b_re, (2, 0, 1)),
                                  jnp.transpose(b_im, (2, 0, 1)))
    eye = jnp.eye(gps, dtype=F32)

    def bd_in(bb):
        bb = jnp.transpose(bb, (1, 0, 2)).reshape(nslab, gps, C, P)
        return jnp.einsum('sgcp,gh->sgchp', bb, eye).reshape(nslab, gps * C, gps * P)

    def bd_out(cc):
        cc = cc.reshape(nslab, gps, C, P)
        return jnp.einsum('sgcp,gh->sgphc', cc, eye).reshape(nslab, gps * P, gps * C)

    bmat = jnp.concatenate([bd_in(bbr), bd_in(bbi)], axis=2).astype(BF16)
    cmat = jnp.concatenate([bd_out(c_re), -bd_out(c_im)], axis=1).astype(BF16)
    return bmat, cmat, abr.reshape(1, G * P), abi.reshape(1, G * P)


def kernel(x, norm1_g, w_in, mu_shift, k_k, k_a, r_k, w0, w_decay_up, a0, w_aaa_up, w_gate_up,
           lnx_w, lnx_b, w_rwkv_proj, a_re, a_im, log_dt, b_re, b_im, c_re, c_im, d_skip, w_glu,
           b_glu, w_out, norm2_g, w_ff_up, w_ff_down, norm_f_g):
    bsz, seq, d = x.shape
    depth = w_in.shape[0]
    n_shift = mu_shift.shape[1]
    n_ssm = d_skip.shape[1]
    t = bsz * seq
    vec = lambda a: a.reshape(a.shape[0], 1, -1)
    bf = lambda a: a.astype(BF16)
    rwkv_params = (vec(mu_shift), vec(k_k), vec(k_a), vec(r_k), vec(w0), bf(w_decay_up), vec(a0),
                   bf(w_aaa_up), bf(w_gate_up), vec(lnx_w), vec(lnx_b), bf(w_rwkv_proj))
    s5_params = jax.vmap(_s5_layer_params)(a_re, a_im, log_dt, b_re, b_im, c_re, c_im)
    s5_params += (vec(d_skip), bf(w_glu), vec(b_glu))
    w_in_bf, w_out_bf, w_up_bf, w_dn_bf = bf(w_in), bf(w_out), bf(w_ff_up), bf(w_ff_down)
    g1, g2 = vec(norm1_g), vec(norm2_g)
    x2 = x.reshape(t, d)
    for l in range(depth):
        z_rwkv, u, gates = _in_proj(x2, g1, w_in_bf, l, n_shift, n_ssm)
        y_a = _rwkv_mix(z_rwkv.reshape(bsz, seq, n_shift), l, *rwkv_params)
        y_b = _s5_mix(u.reshape(bsz, seq, n_ssm), l, *s5_params)
        x2 = _merge_ffn(x2, gates, y_a.reshape(t, d), y_b.reshape(t, d), l, w_out_bf, g2, w_up_bf,
                        w_dn_bf, norm_f_g.reshape(1, -1), final_norm=(l == depth - 1))
    return x2.reshape(bsz, seq, d)
```

```python
import functools
import math

import jax
import jax.numpy as jnp
from jax import lax
from jax.experimental import pallas as pl
from jax.experimental.pallas import tpu as pltpu

F32 = jnp.float32
BF16 = jnp.bfloat16

RWKV_HEAD = 64
DECAY_LORA = 64
AAA_LORA = 64
LNX_EPS = 64e-5
DECAY_SCALE = math.exp(-0.5)
NORM_EPS = 1e-6

CHUNK = 64
VMEM_LIMIT = 56 * 1024 * 1024


def _dotb(a, b):
    return jnp.dot(a.astype(BF16), b.astype(BF16), preferred_element_type=F32)


def _iota2(shape, axis):
    return lax.broadcasted_iota(jnp.int32, shape, axis)


def _layer_spec(arr, layer):
    return pl.BlockSpec((None,) + arr.shape[1:], lambda *_: (layer,) + (0,) * (arr.ndim - 1),
                        pipeline_mode=pl.Buffered(1))


def _in_proj_kernel(x_ref, g_ref, w_ref, zr_ref, u_ref, gt_ref, *, n_shift, n_ssm):
    x = x_ref[...]
    xn = x * lax.rsqrt(jnp.mean(x * x, axis=-1, keepdims=True) + NORM_EPS) * g_ref[...]
    z = _dotb(xn, w_ref[...])
    zr_ref[...] = z[:, :n_shift]
    u_ref[...] = z[:, n_shift:n_shift + n_ssm]
    gt_ref[...] = z[:, n_shift + n_ssm:]


def _in_proj(x2, g, w_bf, layer, n_shift, n_ssm, tm=512):
    t, d = x2.shape
    n_in = w_bf.shape[-1]
    n_gate = n_in - n_shift - n_ssm
    return pl.pallas_call(
        functools.partial(_in_proj_kernel, n_shift=n_shift, n_ssm=n_ssm),
        grid=(t // tm,),
        in_specs=[pl.BlockSpec((tm, d), lambda i: (i, 0)),
                  _layer_spec(g, layer), _layer_spec(w_bf, layer)],
        out_specs=[pl.BlockSpec((tm, n_shift), lambda i: (i, 0)),
                   pl.BlockSpec((tm, n_ssm), lambda i: (i, 0)),
                   pl.BlockSpec((tm, n_gate), lambda i: (i, 0))],
        out_shape=[jax.ShapeDtypeStruct((t, n_shift), F32),
                   jax.ShapeDtypeStruct((t, n_ssm), F32),
                   jax.ShapeDtypeStruct((t, n_gate), F32)],
        compiler_params=pltpu.CompilerParams(dimension_semantics=("parallel",),
                                             vmem_limit_bytes=VMEM_LIMIT),
        name="in_proj",
    )(x2, g, w_bf)


def _dot_exact(m_bf, x, terms, *, m_left):
    acc = None
    rem = x
    for _ in range(terms):
        piece = rem.astype(BF16)
        d = (jnp.dot(m_bf, piece, preferred_element_type=F32) if m_left
             else jnp.dot(piece, m_bf, preferred_element_type=F32))
        acc = d if acc is None else acc + d
        rem = rem - piece.astype(F32)
    return acc


def _segment_sums(ts, n):
    rows, width = ts[0].shape
    tile = 256
    ncol = width // tile
    lg = n.bit_length() - 1
    seg = jnp.where((_iota2((tile, tile), 0) >> lg) == (_iota2((tile, tile), 1) >> lg), 1.0, 0.0).astype(BF16)
    pieces = []
    for t in ts:
        hi = t.astype(BF16)
        lo = (t - hi.astype(F32)).astype(BF16)
        pieces += [part[:, j * tile:(j + 1) * tile] for part in (hi, lo) for j in range(ncol)]
    res = jnp.dot(jnp.concatenate(pieces, axis=0), seg, preferred_element_type=F32)
    blk = lambda i: res[i * rows:(i + 1) * rows]
    return [jnp.concatenate([blk((2 * a) * ncol + j) + blk((2 * a + 1) * ncol + j) for j in range(ncol)], axis=1)
            for a in range(len(ts))]


def _pair_blockdiag(m_cat):
    m = m_cat.astype(BF16)
    m2 = jnp.concatenate([m, m], axis=0)
    keep = (_iota2(m2.shape, 0) >> 6) == (_iota2(m2.shape, 1) >> 6)
    return jnp.where(keep, m2, jnp.zeros_like(m2))


def _pair_dot(a_cat, b_cat=None, b_bd=None):
    b_bd = _pair_blockdiag(b_cat) if b_bd is None else b_bd
    return jnp.dot(a_cat.astype(BF16), b_bd, preferred_element_type=F32)


def _unit_lower_inverse(ns, between=lambda: None):
    L, lanes = ns[0].shape
    ri = _iota2((L, lanes), 0)
    ci = _iota2((L, lanes), 1) & (L - 1)
    eye = (ri == ci).astype(F32)
    same8 = ((ri >> 3) == (ci >> 3)) & (ri > ci)
    n0 = [jnp.where(same8, m, 0.0) for m in ns]
    x = [eye - m for m in n0]
    p = [_pair_dot(m, m) for m in n0]
    between()
    pbd = [_pair_blockdiag(pi) for pi in p]
    x = [xi + _pair_dot(xi, b_bd=pi) for xi, pi in zip(x, pbd)]
    between()
    p = [_pair_dot(pi, b_bd=qi) for pi, qi in zip(p, pbd)]
    between()
    x = [xi + _pair_dot(xi, pi) for xi, pi in zip(x, p)]
    between()
    shift = 4
    while (1 << shift) <= L:
        sel = ((ri >> shift) == (ci >> shift)) & ((ri >> (shift - 1)) > (ci >> (shift - 1)))
        xc = [_pair_dot(xi, jnp.where(sel, m, 0.0)) for xi, m in zip(x, ns)]
        between()
        x = [xi - _pair_dot(xci, xi) for xi, xci in zip(x, xc)]
        between()
        shift += 1
    return x


def _rwkv_kernel(z_ref, mu_ref, kk_ref, ka_ref, rk_ref, w0_ref, wdu_ref, a0_ref, wau_ref,
                 wgu_ref, lnw_ref, lnb_ref, wproj_ref, out_ref, carry_ref, state_ref, y_scr,
                 *staging, width, nblk):
    step = pl.program_id(0)
    nstage = len(staging) // 2

    @pl.when(step == 0)
    def _():
        for ref in (carry_ref, state_ref) + tuple(staging):
            ref[...] = jnp.zeros_like(ref)

    for parity in range(2):
        @pl.when((step & 1) == parity)
        def _(parity=parity):
            _rwkv_step(z_ref, mu_ref, kk_ref, ka_ref, rk_ref, w0_ref, wdu_ref, a0_ref, wau_ref, wgu_ref,
                       lnw_ref, lnb_ref, wproj_ref, out_ref, carry_ref, state_ref, y_scr,
                       staging[parity * nstage:(parity + 1) * nstage],
                       staging[(1 - parity) * nstage:(2 - parity) * nstage], step, width, nblk)


def _rwkv_step(z_ref, mu_ref, kk_ref, ka_ref, rk_ref, w0_ref, wdu_ref, a0_ref, wau_ref, wgu_ref,
               lnw_ref, lnb_ref, wproj_ref, out_ref, carry_ref, state_ref, y_scr, cur, prev, step, width, nblk):
    lhs1_w, rhs1_w, lhs2_w, rt_w, v_w, glast_w, g_w, bonus_w = cur
    lhs1_s, rhs1_s, lhs2_s, rt_s, v_s, glast_s, g_s, bonus_s = prev
    rows = z_ref.shape[0]
    n = RWKV_HEAD
    L = CHUNK
    nchunks = rows // L
    W = width
    PW = 2 * n
    npairs = W // PW
    psl = lambda p: slice(p * PW, (p + 1) * PW)
    csl = lambda c: slice(c * L, (c + 1) * L)

    t = {}

    seq_start = lax.rem(step, nblk) == 0

    def shift_mix(lo, hi):
        z = z_ref[:, lo:hi]
        row_id = _iota2((rows, 1), 0)
        carry = jnp.where(seq_start, 0.0, carry_ref[0:1, lo:hi])
        zprev = jnp.where(row_id == 0, carry, pltpu.roll(z, 1, axis=0))
        carry_ref[0:1, lo:hi] = z[rows - 1:rows, :]
        return z + (zprev - z) * mu_ref[:, lo:hi]

    def task_r():
        t['r'] = shift_mix(0, W)

    def task_k():
        t['k'] = shift_mix(W, 2 * W)

    def task_v():
        t['v'] = shift_mix(2 * W, 3 * W)
        v_w[...] = t['v']

    def task_lora():
        zs = shift_mix(3 * W, z_ref.shape[1])
        wd = zs[:, :DECAY_LORA]
        ad = zs[:, DECAY_LORA:DECAY_LORA + AAA_LORA]
        gd = zs[:, DECAY_LORA + AAA_LORA:]
        t['lw'] = -DECAY_SCALE * jax.nn.sigmoid(w0_ref[...] + _dotb(jnp.tanh(wd), wdu_ref[...]))
        t['a'] = jax.nn.sigmoid(a0_ref[...] + _dotb(ad, wau_ref[...]))
        g_w[...] = _dotb(jax.nn.sigmoid(gd), wgu_ref[...])

    def task_kk():
        t['kk'] = t['k'] * kk_ref[...]
        t['k_mod'] = t['k'] * (1.0 + (t['a'] - 1.0) * ka_ref[...])

    def task_norm():
        kk_sq, rk_sum = _segment_sums([t['kk'] * t['kk'], t['r'] * t['k_mod'] * rk_ref[...]], n)
        t['kk'] = t['kk'] / jnp.maximum(jnp.sqrt(kk_sq), 1e-12)
        t['b'] = t['kk'] * t['a']
        bonus_w[...] = rk_sum * t['v']

    tri_bf = jnp.where(_iota2((L, L), 0) >= _iota2((L, L), 1), 1.0, 0.0).astype(BF16)

    def task_decay(c):
        def run():
            sl = csl(c)
            lw_c = t['lw'][sl]
            cum = _dot_exact(tri_bf, lw_c, 3, m_left=True)
            cum_last = cum[L - 1:L, :]
            t['g_inv', c] = jnp.exp(-cum)
            t['g_end', c] = jnp.exp(cum_last - cum)
            glast_w[c] = jnp.exp(cum_last)
            rt = t['r'][sl] * jnp.exp(cum)
            at = t['kk'][sl] * jnp.exp(cum - lw_c)
            rt_w[sl] = rt
            lhs1_w[c] = jnp.concatenate([at, rt], axis=0).astype(BF16)
        return run

    def dup_t(t1, t2):
        keep = (_iota2((4 * L, W), 0) >> 6 & 1) == (_iota2((4 * L, W), 1) >> 6 & 1)
        return jnp.where(keep, jnp.concatenate([t1, t1, t2, t2], axis=0), 0.0).T

    def task_rhs1(c):
        def run():
            sl = csl(c)
            rhs1_w[c] = dup_t(t['b'][sl] * t['g_inv', c], t['k_mod'][sl] * t['g_inv', c]).astype(BF16)
        return run

    def task_lhs2(c):
        def run():
            sl = csl(c)
            bk = dup_t(t['b'][sl] * t['g_end', c], t['k_mod'][sl] * t['g_end', c])
            for p in range(npairs):
                lhs2_w[c, :, p * 2 * PW:(p + 1) * 2 * PW] = (
                    bk[p * PW:p * PW + n] + bk[p * PW + n:(p + 1) * PW]).astype(BF16)
        return run

    tasks = [task_r, task_k, task_v, task_lora, task_kk, task_norm]
    for c in range(nchunks):
        tasks += [task_decay(c), task_rhs1(c), task_lhs2(c)]
    tasks = iter(tasks)

    def tick(k=1):
        for _ in range(k):
            task = next(tasks, None)
            if task is not None:
                task()

    ri = _iota2((L, 2 * PW), 0)
    ci = _iota2((L, 2 * PW), 1) & (L - 1)
    strict = (ri > ci)[:, :PW]
    incl = ri >= ci
    eye_cat = (_iota2((n, PW), 0) == (_iota2((n, PW), 1) & (n - 1))).astype(F32)
    zeros_bd = jnp.zeros((PW, PW), BF16)
    units = [(c, p) for c in range(nchunks) for p in range(npairs)]
    x1 = [jnp.dot(lhs1_s[c, :, psl(p)], rhs1_s[c, psl(p), :], preferred_element_type=F32)
          for c, p in units]
    tick()
    t_inv = _unit_lower_inverse([m[:L, :PW] for m in x1], between=tick)
    vbd = [_pair_blockdiag(v_s[csl(c), psl(p)]) for c, p in units]
    zero_bf = jnp.zeros((), BF16)
    nv = [_pair_dot(jnp.where(strict, m[:L, PW:].astype(BF16), zero_bf), b_bd=vb) for m, vb in zip(x1, vbd)]
    tick()
    pq = [jnp.dot(tm.astype(BF16),
                  jnp.concatenate([_pair_blockdiag(lhs1_s[c, :L, psl(p)]), _pair_blockdiag(-q)], axis=1),
                  preferred_element_type=F32) for tm, q, (c, p) in zip(t_inv, nv, units)]
    tick()
    r2 = []
    for m, pqi, vb, (c, p) in zip(x1, pq, vbd, units):
        rhs2 = jnp.concatenate(
            [jnp.concatenate([_pair_blockdiag(pqi[:, :PW]), _pair_blockdiag(pqi[:, PW:])], axis=1),
             jnp.concatenate([zeros_bd, vb], axis=1)], axis=0)
        l2 = jnp.concatenate([jnp.where(incl, m[L:, :].astype(BF16), zero_bf),
                              lhs2_s[c, :, p * 2 * PW:(p + 1) * 2 * PW]], axis=0)
        r2.append(jnp.dot(l2, rhs2, preferred_element_type=F32))
    tick()
    lhs3 = [jnp.concatenate([rt_s[csl(c), psl(p)] - m[:L, :PW],
                             eye_cat * glast_s[c][:, psl(p)] - m[L:, :PW]], axis=0)
            for m, (c, p) in zip(r2, units)]

    first = lax.rem(step + nblk - 1, nblk) == 0
    state = [jnp.where(first, 0.0, state_ref[p]) for p in range(npairs)]
    for c in range(nchunks):
        r3 = [_pair_dot(lhs3[c * npairs + p], state[p]) for p in range(npairs)]
        for p in range(npairs):
            m = r2[c * npairs + p]
            y_scr[csl(c), psl(p)] = r3[p][:L] + m[:L, PW:]
            state[p] = r3[p][L:] + m[L:, PW:]
        tick()
    for p in range(npairs):
        state_ref[p] = state[p]
    tick(len(units))

    y = y_scr[...]
    inv_n = 1.0 / n
    yc = y - _segment_sums([y], n)[0] * inv_n
    var = _segment_sums([yc * yc], n)[0] * inv_n
    yn = yc * lax.rsqrt(var + LNX_EPS) * lnw_ref[...] + lnb_ref[...]
    out_ref[...] = _dotb((yn + bonus_s[...]) * g_s[...], wproj_ref[...])


def _rwkv_mix(z_rwkv, seq, layer, mu, k_k, k_a, r_k, w0, wdu, a0, wau, wgu, lnw, lnb, wproj, rows=256):
    t, ncols = z_rwkv.shape
    W = k_k.shape[-1]
    heads = W // RWKV_HEAD
    d = wproj.shape[-1]
    nblk = seq // rows
    nsteps = t // rows
    nchunks = rows // CHUNK
    full = lambda arr: _layer_spec(arr, layer)
    params = (mu, k_k, k_a, r_k, w0, wdu, a0, wau, wgu, lnw, lnb, wproj)
    return pl.pallas_call(
        functools.partial(_rwkv_kernel, width=W, nblk=nblk),
        grid=(nsteps + 1,),
        in_specs=[pl.BlockSpec((rows, ncols), lambda s: (jnp.minimum(s, nsteps - 1), 0))]
        + [full(p) for p in params],
        out_specs=pl.BlockSpec((rows, d), lambda s: (jnp.maximum(s - 1, 0), 0)),
        out_shape=jax.ShapeDtypeStruct((t, d), F32),
        scratch_shapes=[pltpu.VMEM((8, ncols), F32),
                        pltpu.VMEM((heads // 2, RWKV_HEAD, 2 * RWKV_HEAD), F32),
                        pltpu.VMEM((rows, W), F32)] + 2 * [
                        pltpu.VMEM((nchunks, 2 * CHUNK, W), BF16),
                        pltpu.VMEM((nchunks, W, 4 * CHUNK), BF16),
                        pltpu.VMEM((nchunks, RWKV_HEAD, 2 * W), BF16),
                        pltpu.VMEM((rows, W), F32),
                        pltpu.VMEM((rows, W), F32),
                        pltpu.VMEM((nchunks, 1, W), F32),
                        pltpu.VMEM((rows, W), F32),
                        pltpu.VMEM((rows, W), F32)],
        compiler_params=pltpu.CompilerParams(dimension_semantics=("arbitrary",),
                                             vmem_limit_bytes=VMEM_LIMIT),
        name="rwkv_mix",
    )(z_rwkv, *params)


def _s5_disc_kernel(are_ref, aim_ref, ldt_ref, bre_ref, bim_ref, abr_ref, abi_ref, bbr_ref, bbi_ref):
    dt = jnp.exp(ldt_ref[...])
    are = jnp.minimum(are_ref[...], -1e-4)
    aim = aim_ref[...]
    mag = jnp.exp(dt * are)
    abr = mag * jnp.cos(dt * aim)
    abi = mag * jnp.sin(dt * aim)
    den = are * are + aim * aim
    nr = abr - 1.0
    cre = (nr * are + abi * aim) / den
    cim = (abi * are - nr * aim) / den
    abr_ref[...] = abr
    abi_ref[...] = abi
    br = bre_ref[...]
    bi = bim_ref[...]
    bbr_ref[...] = cre[None] * br - cim[None] * bi
    bbi_ref[...] = cre[None] * bi + cim[None] * br


def _s5_disc(a_re, a_im, log_dt, b_re_c, b_im_c):
    G, P = a_re.shape
    C = b_re_c.shape[0]
    return pl.pallas_call(
        _s5_disc_kernel,
        out_shape=[jax.ShapeDtypeStruct((G, P), F32), jax.ShapeDtypeStruct((G, P), F32),
                   jax.ShapeDtypeStruct((C, G, P), F32), jax.ShapeDtypeStruct((C, G, P), F32)],
        name="s5_disc",
    )(a_re, a_im, log_dt.reshape(G, 1), b_re_c, b_im_c)


def _s5_kernel(u_ref, bmat_ref, cmat_ref, abr_ref, abi_ref, dskip_ref, wglu_ref, bglu_ref,
               out_ref, st_r_ref, st_i_ref, x_scr, *, nslab):
    nb, rt, wc = u_ref.shape
    rows = nb * rt
    ns = abr_ref.shape[1]
    sw = ns // nslab
    d = out_ref.shape[2]
    lg_nb, lg_rt = nb.bit_length() - 1, rt.bit_length() - 1

    @pl.when(pl.program_id(1) == 0)
    def _():
        st_r_ref[...] = jnp.zeros_like(st_r_ref)
        st_i_ref[...] = jnp.zeros_like(st_i_ref)

    u = u_ref[...].reshape(rows, wc)
    ri = _iota2((rows, rows), 0)
    ci = _iota2((rows, rows), 1)
    to_tb = jnp.where(ci == ((ri & (nb - 1)) << lg_rt) + (ri >> lg_nb), 1.0, 0.0).astype(BF16)
    to_bt = jnp.where(ci == ((ri & (rt - 1)) << lg_nb) + (ri >> lg_rt), 1.0, 0.0).astype(BF16)
    u_tb = jnp.dot(to_tb, u.astype(BF16), preferred_element_type=F32).astype(BF16)
    ys = []
    for s in range(nslab):
        bu = jnp.dot(u_tb[:, s * 128:(s + 1) * 128], bmat_ref[s], preferred_element_type=F32)
        cs = slice(s * sw, (s + 1) * sw)
        ar = abr_ref[:, cs]
        ai = abi_ref[:, cs]
        xr = st_r_ref[:, cs]
        xi = st_i_ref[:, cs]
        for t in range(rt):
            ts = slice(t * nb, (t + 1) * nb)
            xr, xi = ar * xr - ai * xi + bu[ts, :sw], ar * xi + ai * xr + bu[ts, sw:]
            x_scr[ts, :sw] = xr
            x_scr[ts, sw:] = xi
        st_r_ref[:, cs] = xr
        st_i_ref[:, cs] = xi
        ys.append(_dotb(x_scr[...], cmat_ref[s]))
    y = _dot_exact(to_bt, jnp.concatenate(ys, axis=-1), 2, m_left=True) + dskip_ref[...] * u
    y = 0.5 * y * (1.0 + lax.erf(y * (1.0 / math.sqrt(2.0))))
    zz = _dotb(y, wglu_ref[...]) + bglu_ref[...]
    out_ref[...] = (zz[:, :d] * jax.nn.sigmoid(zz[:, d:])).reshape(nb, rt, d)


def _s5_mix(u3, layer, bmat, cmat, abr, abi, dskip, wglu, bglu, nb=8, rt=64):
    bsz, seq, wc = u3.shape
    assert nb & (nb - 1) == 0 and rt & (rt - 1) == 0 and rt % 8 == 0
    nslab = bmat.shape[1]
    ns = abr.shape[-1]
    d = wglu.shape[-1] // 2
    full = lambda arr: _layer_spec(arr, layer)
    params = (bmat, cmat, abr, abi, dskip, wglu, bglu)
    return pl.pallas_call(
        functools.partial(_s5_kernel, nslab=nslab),
        grid=(bsz // nb, seq // rt),
        in_specs=[pl.BlockSpec((nb, rt, wc), lambda b, j: (b, j, 0))] + [full(p) for p in params],
        out_specs=pl.BlockSpec((nb, rt, d), lambda b, j: (b, j, 0)),
        out_shape=jax.ShapeDtypeStruct((bsz, seq, d), F32),
        scratch_shapes=[pltpu.VMEM((nb, ns), F32), pltpu.VMEM((nb, ns), F32),
                        pltpu.VMEM((nb * rt, 2 * ns // nslab), F32)],
        compiler_params=pltpu.CompilerParams(dimension_semantics=("parallel", "arbitrary"),
                                             vmem_limit_bytes=VMEM_LIMIT),
        name="s5_mix",
    )(u3, *params)


def _merge_ffn_kernel(x_ref, gt_ref, ya_ref, yb_ref, wout_ref, g2_ref, wup_ref, wdn_ref, gf_ref,
                      out_ref, *, final_norm, ff_chunk):
    d = x_ref.shape[1]
    gt = jax.nn.sigmoid(gt_ref[...])
    m = gt[:, :d] * ya_ref[...] + gt[:, d:] * yb_ref[...]
    x1 = x_ref[...] + _dotb(m, wout_ref[...])
    xn = x1 * lax.rsqrt(jnp.mean(x1 * x1, axis=-1, keepdims=True) + NORM_EPS) * g2_ref[...]
    xn = xn.astype(BF16)
    acc = x1
    dff = wup_ref.shape[1]
    for c in range(dff // ff_chunk):
        cs = slice(c * ff_chunk, (c + 1) * ff_chunk)
        h = jnp.dot(xn, wup_ref[:, cs], preferred_element_type=F32)
        h = jnp.square(jnp.maximum(h, 0.0))
        acc = acc + _dotb(h, wdn_ref[cs, :])
    if final_norm:
        acc = acc * lax.rsqrt(jnp.mean(acc * acc, axis=-1, keepdims=True) + NORM_EPS) * gf_ref[...]
    out_ref[...] = acc


def _merge_ffn(x2, gates, ya, yb, layer, wout, g2, wup, wdn, gf, final_norm, tm=512, ff_chunk=1024):
    t, d = x2.shape
    row = lambda w: pl.BlockSpec((tm, w), lambda i: (i, 0))
    full = lambda arr: _layer_spec(arr, layer)
    return pl.pallas_call(
        functools.partial(_merge_ffn_kernel, final_norm=final_norm, ff_chunk=ff_chunk),
        grid=(t // tm,),
        in_specs=[row(d), row(2 * d), row(d), row(d), full(wout), full(g2), full(wup), full(wdn),
                  pl.BlockSpec(gf.shape, lambda i: (0, 0))],
        out_specs=row(d),
        out_shape=jax.ShapeDtypeStruct((t, d), F32),
        compiler_params=pltpu.CompilerParams(dimension_semantics=("parallel",),
                                             vmem_limit_bytes=VMEM_LIMIT),
        name="merge_ffn",
    )(x2, gates, ya, yb, wout, g2, wup, wdn, gf)


def _s5_layer_params(a_re, a_im, log_dt, b_re, b_im, c_re, c_im):
    G, P, C = b_re.shape
    gps = 128 // C
    nslab = G // gps
    abr, abi, bbr, bbi = _s5_disc(a_re, a_im, log_dt, jnp.transpose(b_re, (2, 0, 1)),
                                  jnp.transpose(b_im, (2, 0, 1)))
    eye = jnp.eye(gps, dtype=F32)

    def bd_in(bb):
        bb = jnp.transpose(bb, (1, 0, 2)).reshape(nslab, gps, C, P)
        return jnp.einsum('sgcp,gh->sgchp', bb, eye).reshape(nslab, gps * C, gps * P)

    def bd_out(cc):
        cc = cc.reshape(nslab, gps, C, P)
        return jnp.einsum('sgcp,gh->sgphc', cc, eye).reshape(nslab, gps * P, gps * C)

    bmat = jnp.concatenate([bd_in(bbr), bd_in(bbi)], axis=2).astype(BF16)
    cmat = jnp.concatenate([bd_out(c_re), -bd_out(c_im)], axis=1).astype(BF16)
    return bmat, cmat, abr.reshape(1, G * P), abi.reshape(1, G * P)


def kernel(x, norm1_g, w_in, mu_shift, k_k, k_a, r_k, w0, w_decay_up, a0, w_aaa_up, w_gate_up,
           lnx_w, lnx_b, w_rwkv_proj, a_re, a_im, log_dt, b_re, b_im, c_re, c_im, d_skip, w_glu,
           b_glu, w_out, norm2_g, w_ff_up, w_ff_down, norm_f_g):
    bsz, seq, d = x.shape
    depth = w_in.shape[0]
    n_shift = mu_shift.shape[1]
    n_ssm = d_skip.shape[1]
    t = bsz * seq
    vec = lambda a: a.reshape(a.shape[0], 1, -1)
    bf = lambda a: a.astype(BF16)
    rwkv_params = (vec(mu_shift), vec(k_k), vec(k_a), vec(r_k), vec(w0), bf(w_decay_up), vec(a0),
                   bf(w_aaa_up), bf(w_gate_up), vec(lnx_w), vec(lnx_b), bf(w_rwkv_proj))
    s5_params = jax.vmap(_s5_layer_params)(a_re, a_im, log_dt, b_re, b_im, c_re, c_im)
    s5_params += (vec(d_skip), bf(w_glu), vec(b_glu))
    w_in_bf, w_out_bf, w_up_bf, w_dn_bf = bf(w_in), bf(w_out), bf(w_ff_up), bf(w_ff_down)
    g1, g2 = vec(norm1_g), vec(norm2_g)
    x2 = x.reshape(t, d)
    for l in range(depth):
        z_rwkv, u, gates = _in_proj(x2, g1, w_in_bf, l, n_shift, n_ssm)
        y_a = _rwkv_mix(z_rwkv, seq, l, *rwkv_params)
        y_b = _s5_mix(u.reshape(bsz, seq, n_ssm), l, *s5_params)
        x2 = _merge_ffn(x2, gates, y_a, y_b.reshape(t, d), l, w_out_bf, g2, w_up_bf,
                        w_dn_bf, norm_f_g.reshape(1, -1), final_norm=(l == depth - 1))
    return x2.reshape(bsz, seq, d)
```

```python
import functools
import math

import jax
import jax.numpy as jnp
from jax import lax
from jax.experimental import pallas as pl
from jax.experimental.pallas import tpu as pltpu

F32 = jnp.float32
BF16 = jnp.bfloat16

RWKV_HEAD = 64
DECAY_LORA = 64
AAA_LORA = 64
LNX_EPS = 64e-5
DECAY_SCALE = math.exp(-0.5)
NORM_EPS = 1e-6

CHUNK = 64
VMEM_LIMIT = 56 * 1024 * 1024


def _dotb(a, b):
    return jnp.dot(a.astype(BF16), b.astype(BF16), preferred_element_type=F32)


def _iota2(shape, axis):
    return lax.broadcasted_iota(jnp.int32, shape, axis)


def _layer_spec(arr, layer):
    return pl.BlockSpec((None,) + arr.shape[1:], lambda *_: (layer,) + (0,) * (arr.ndim - 1),
                        pipeline_mode=pl.Buffered(1))


def _in_proj_kernel(x_ref, g_ref, w_ref, zr_ref, u_ref, gt_ref, *, n_shift, n_ssm):
    x = x_ref[...]
    xn = x * lax.rsqrt(jnp.mean(x * x, axis=-1, keepdims=True) + NORM_EPS) * g_ref[...]
    z = _dotb(xn, w_ref[...])
    zr_ref[...] = z[:, :n_shift]
    for s in range(u_ref.shape[0]):
        u_ref[s] = z[:, n_shift + s * 128:n_shift + (s + 1) * 128]
    gt_ref[...] = z[:, n_shift + n_ssm:]


def _in_proj(x2, g, w_bf, layer, n_shift, n_ssm, tm=512):
    t, d = x2.shape
    n_in = w_bf.shape[-1]
    n_gate = n_in - n_shift - n_ssm
    return pl.pallas_call(
        functools.partial(_in_proj_kernel, n_shift=n_shift, n_ssm=n_ssm),
        grid=(t // tm,),
        in_specs=[pl.BlockSpec((tm, d), lambda i: (i, 0)),
                  _layer_spec(g, layer), _layer_spec(w_bf, layer)],
        out_specs=[pl.BlockSpec((tm, n_shift), lambda i: (i, 0)),
                   pl.BlockSpec((n_ssm // 128, tm, 128), lambda i: (0, i, 0)),
                   pl.BlockSpec((tm, n_gate), lambda i: (i, 0))],
        out_shape=[jax.ShapeDtypeStruct((t, n_shift), F32),
                   jax.ShapeDtypeStruct((n_ssm // 128, t, 128), F32),
                   jax.ShapeDtypeStruct((t, n_gate), F32)],
        compiler_params=pltpu.CompilerParams(dimension_semantics=("parallel",),
                                             vmem_limit_bytes=VMEM_LIMIT),
        name="in_proj",
    )(x2, g, w_bf)


def _dot_exact(m_bf, x, terms, *, m_left):
    acc = None
    rem = x
    for _ in range(terms):
        piece = rem.astype(BF16)
        d = (jnp.dot(m_bf, piece, preferred_element_type=F32) if m_left
             else jnp.dot(piece, m_bf, preferred_element_type=F32))
        acc = d if acc is None else acc + d
        rem = rem - piece.astype(F32)
    return acc


def _segment_sums(ts, n):
    rows, width = ts[0].shape
    tile = 256
    ncol = width // tile
    lg = n.bit_length() - 1
    seg = jnp.where((_iota2((tile, tile), 0) >> lg) == (_iota2((tile, tile), 1) >> lg), 1.0, 0.0).astype(BF16)
    pieces = []
    for t in ts:
        hi = t.astype(BF16)
        lo = (t - hi.astype(F32)).astype(BF16)
        pieces += [part[:, j * tile:(j + 1) * tile] for part in (hi, lo) for j in range(ncol)]
    res = jnp.dot(jnp.concatenate(pieces, axis=0), seg, preferred_element_type=F32)
    blk = lambda i: res[i * rows:(i + 1) * rows]
    return [jnp.concatenate([blk((2 * a) * ncol + j) + blk((2 * a + 1) * ncol + j) for j in range(ncol)], axis=1)
            for a in range(len(ts))]


def _pair_blockdiag(m_cat):
    m = m_cat.astype(BF16)
    m2 = jnp.concatenate([m, m], axis=0)
    keep = (_iota2(m2.shape, 0) >> 6) == (_iota2(m2.shape, 1) >> 6)
    return jnp.where(keep, m2, jnp.zeros_like(m2))


def _pair_dot(a_cat, b_cat=None, b_bd=None):
    b_bd = _pair_blockdiag(b_cat) if b_bd is None else b_bd
    return jnp.dot(a_cat.astype(BF16), b_bd, preferred_element_type=F32)


def _unit_lower_inverse(ns, between=lambda: None):
    L, lanes = ns[0].shape
    ri = _iota2((L, lanes), 0)
    ci = _iota2((L, lanes), 1) & (L - 1)
    eye = (ri == ci).astype(F32)
    same8 = ((ri >> 3) == (ci >> 3)) & (ri > ci)
    n0 = [jnp.where(same8, m, 0.0) for m in ns]
    x = [eye - m for m in n0]
    p = [_pair_dot(m, m) for m in n0]
    between()
    pbd = [_pair_blockdiag(pi) for pi in p]
    x = [xi + _pair_dot(xi, b_bd=pi) for xi, pi in zip(x, pbd)]
    between()
    p = [_pair_dot(pi, b_bd=qi) for pi, qi in zip(p, pbd)]
    between()
    x = [xi + _pair_dot(xi, pi) for xi, pi in zip(x, p)]
    between()
    shift = 4
    while (1 << shift) <= L:
        sel = ((ri >> shift) == (ci >> shift)) & ((ri >> (shift - 1)) > (ci >> (shift - 1)))
        xc = [_pair_dot(xi, jnp.where(sel, m, 0.0)) for xi, m in zip(x, ns)]
        between()
        x = [xi - _pair_dot(xci, xi) for xi, xci in zip(x, xc)]
        between()
        shift += 1
    return x


def _rwkv_kernel(z_ref, mu_ref, kk_ref, ka_ref, rk_ref, w0_ref, wdu_ref, a0_ref, wau_ref,
                 wgu_ref, lnw_ref, lnb_ref, wproj_ref, out_ref, carry_ref, state_ref, y_scr,
                 *staging, width, nblk):
    step = pl.program_id(0)
    nstage = len(staging) // 2

    @pl.when(step == 0)
    def _():
        for ref in (carry_ref, state_ref) + tuple(staging):
            ref[...] = jnp.zeros_like(ref)

    for parity in range(2):
        @pl.when((step & 1) == parity)
        def _(parity=parity):
            _rwkv_step(z_ref, mu_ref, kk_ref, ka_ref, rk_ref, w0_ref, wdu_ref, a0_ref, wau_ref, wgu_ref,
                       lnw_ref, lnb_ref, wproj_ref, out_ref, carry_ref, state_ref, y_scr,
                       staging[parity * nstage:(parity + 1) * nstage],
                       staging[(1 - parity) * nstage:(2 - parity) * nstage], step, width, nblk)


def _rwkv_step(z_ref, mu_ref, kk_ref, ka_ref, rk_ref, w0_ref, wdu_ref, a0_ref, wau_ref, wgu_ref,
               lnw_ref, lnb_ref, wproj_ref, out_ref, carry_ref, state_ref, y_scr, cur, prev, step, width, nblk):
    lhs1_w, rhs1_w, lhs2_w, rt_w, v_w, glast_w, g_w, bonus_w = cur
    lhs1_s, rhs1_s, lhs2_s, rt_s, v_s, glast_s, g_s, bonus_s = prev
    rows = z_ref.shape[0]
    n = RWKV_HEAD
    L = CHUNK
    nchunks = rows // L
    W = width
    PW = 2 * n
    npairs = W // PW
    psl = lambda p: slice(p * PW, (p + 1) * PW)
    csl = lambda c: slice(c * L, (c + 1) * L)

    t = {}

    seq_start = lax.rem(step, nblk) == 0

    def shift_mix(lo, hi):
        z = z_ref[:, lo:hi]
        row_id = _iota2((rows, 1), 0)
        carry = jnp.where(seq_start, 0.0, carry_ref[0:1, lo:hi])
        zprev = jnp.where(row_id == 0, carry, pltpu.roll(z, 1, axis=0))
        carry_ref[0:1, lo:hi] = z[rows - 1:rows, :]
        return z + (zprev - z) * mu_ref[:, lo:hi]

    def task_r():
        t['r'] = shift_mix(0, W)

    def task_k():
        t['k'] = shift_mix(W, 2 * W)

    def task_v():
        t['v'] = shift_mix(2 * W, 3 * W)
        v_w[...] = t['v']

    def task_lora():
        zs = shift_mix(3 * W, z_ref.shape[1])
        wd = zs[:, :DECAY_LORA]
        ad = zs[:, DECAY_LORA:DECAY_LORA + AAA_LORA]
        gd = zs[:, DECAY_LORA + AAA_LORA:]
        t['lw'] = -DECAY_SCALE * jax.nn.sigmoid(w0_ref[...] + _dotb(jnp.tanh(wd), wdu_ref[...]))
        t['a'] = jax.nn.sigmoid(a0_ref[...] + _dotb(ad, wau_ref[...]))
        g_w[...] = _dotb(jax.nn.sigmoid(gd), wgu_ref[...])

    def task_kk():
        t['kk'] = t['k'] * kk_ref[...]
        t['k_mod'] = t['k'] * (1.0 + (t['a'] - 1.0) * ka_ref[...])

    def task_norm():
        kk_sq, rk_sum = _segment_sums([t['kk'] * t['kk'], t['r'] * t['k_mod'] * rk_ref[...]], n)
        t['kk'] = t['kk'] / jnp.maximum(jnp.sqrt(kk_sq), 1e-12)
        t['b'] = t['kk'] * t['a']
        bonus_w[...] = rk_sum * t['v']

    tri_bf = jnp.where(_iota2((L, L), 0) >= _iota2((L, L), 1), 1.0, 0.0).astype(BF16)

    def task_decay(c):
        def run():
            sl = csl(c)
            lw_c = t['lw'][sl]
            cum = _dot_exact(tri_bf, lw_c, 3, m_left=True)
            cum_last = cum[L - 1:L, :]
            t['g_inv', c] = jnp.exp(-cum)
            t['g_end', c] = jnp.exp(cum_last - cum)
            glast_w[c] = jnp.exp(cum_last)
            rt = t['r'][sl] * jnp.exp(cum)
            at = t['kk'][sl] * jnp.exp(cum - lw_c)
            rt_w[sl] = rt
            lhs1_w[c] = jnp.concatenate([at, rt], axis=0).astype(BF16)
        return run

    def dup_t(t1, t2):
        keep = (_iota2((4 * L, W), 0) >> 6 & 1) == (_iota2((4 * L, W), 1) >> 6 & 1)
        return jnp.where(keep, jnp.concatenate([t1, t1, t2, t2], axis=0), 0.0).T

    def task_rhs1(c):
        def run():
            sl = csl(c)
            rhs1_w[c] = dup_t(t['b'][sl] * t['g_inv', c], t['k_mod'][sl] * t['g_inv', c]).astype(BF16)
        return run

    def task_lhs2(c):
        def run():
            sl = csl(c)
            bk = dup_t(t['b'][sl] * t['g_end', c], t['k_mod'][sl] * t['g_end', c])
            for p in range(npairs):
                lhs2_w[c, :, p * 2 * PW:(p + 1) * 2 * PW] = (
                    bk[p * PW:p * PW + n] + bk[p * PW + n:(p + 1) * PW]).astype(BF16)
        return run

    tasks = [task_r, task_k, task_v, task_lora, task_kk, task_norm]
    for c in range(nchunks):
        tasks += [task_decay(c), task_rhs1(c), task_lhs2(c)]
    tasks = iter(tasks)

    def tick(k=1):
        for _ in range(k):
            task = next(tasks, None)
            if task is not None:
                task()

    ri = _iota2((L, 2 * PW), 0)
    ci = _iota2((L, 2 * PW), 1) & (L - 1)
    strict = (ri > ci)[:, :PW]
    incl = ri >= ci
    eye_cat = (_iota2((n, PW), 0) == (_iota2((n, PW), 1) & (n - 1))).astype(F32)
    zeros_bd = jnp.zeros((PW, PW), BF16)
    units = [(c, p) for c in range(nchunks) for p in range(npairs)]
    x1 = [jnp.dot(lhs1_s[c, :, psl(p)], rhs1_s[c, psl(p), :], preferred_element_type=F32)
          for c, p in units]
    tick()
    t_inv = _unit_lower_inverse([m[:L, :PW] for m in x1], between=tick)
    vbd = [_pair_blockdiag(v_s[csl(c), psl(p)]) for c, p in units]
    zero_bf = jnp.zeros((), BF16)
    nv = [_pair_dot(jnp.where(strict, m[:L, PW:].astype(BF16), zero_bf), b_bd=vb) for m, vb in zip(x1, vbd)]
    tick()
    pq = [jnp.dot(tm.astype(BF16),
                  jnp.concatenate([_pair_blockdiag(lhs1_s[c, :L, psl(p)]), _pair_blockdiag(-q)], axis=1),
                  preferred_element_type=F32) for tm, q, (c, p) in zip(t_inv, nv, units)]
    tick()
    r2 = []
    for m, pqi, vb, (c, p) in zip(x1, pq, vbd, units):
        rhs2 = jnp.concatenate(
            [jnp.concatenate([_pair_blockdiag(pqi[:, :PW]), _pair_blockdiag(pqi[:, PW:])], axis=1),
             jnp.concatenate([zeros_bd, vb], axis=1)], axis=0)
        l2 = jnp.concatenate([jnp.where(incl, m[L:, :].astype(BF16), zero_bf),
                              lhs2_s[c, :, p * 2 * PW:(p + 1) * 2 * PW]], axis=0)
        r2.append(jnp.dot(l2, rhs2, preferred_element_type=F32))
    tick()
    lhs3 = [jnp.concatenate([rt_s[csl(c), psl(p)] - m[:L, :PW],
                             eye_cat * glast_s[c][:, psl(p)] - m[L:, :PW]], axis=0)
            for m, (c, p) in zip(r2, units)]

    first = lax.rem(step + nblk - 1, nblk) == 0
    state = [jnp.where(first, 0.0, state_ref[p]) for p in range(npairs)]
    for c in range(nchunks):
        r3 = [_pair_dot(lhs3[c * npairs + p], state[p]) for p in range(npairs)]
        for p in range(npairs):
            m = r2[c * npairs + p]
            y_scr[csl(c), psl(p)] = r3[p][:L] + m[:L, PW:]
            state[p] = r3[p][L:] + m[L:, PW:]
        tick()
    for p in range(npairs):
        state_ref[p] = state[p]
    tick(len(units))

    y = y_scr[...]
    inv_n = 1.0 / n
    yc = y - _segment_sums([y], n)[0] * inv_n
    var = _segment_sums([yc * yc], n)[0] * inv_n
    yn = yc * lax.rsqrt(var + LNX_EPS) * lnw_ref[...] + lnb_ref[...]
    out_ref[...] = _dotb((yn + bonus_s[...]) * g_s[...], wproj_ref[...])


def _rwkv_mix(z_rwkv, seq, layer, mu, k_k, k_a, r_k, w0, wdu, a0, wau, wgu, lnw, lnb, wproj, rows=256):
    t, ncols = z_rwkv.shape
    W = k_k.shape[-1]
    heads = W // RWKV_HEAD
    d = wproj.shape[-1]
    nblk = seq // rows
    nsteps = t // rows
    nchunks = rows // CHUNK
    full = lambda arr: _layer_spec(arr, layer)
    params = (mu, k_k, k_a, r_k, w0, wdu, a0, wau, wgu, lnw, lnb, wproj)
    return pl.pallas_call(
        functools.partial(_rwkv_kernel, width=W, nblk=nblk),
        grid=(nsteps + 1,),
        in_specs=[pl.BlockSpec((rows, ncols), lambda s: (jnp.minimum(s, nsteps - 1), 0))]
        + [full(p) for p in params],
        out_specs=pl.BlockSpec((rows, d), lambda s: (jnp.maximum(s - 1, 0), 0)),
        out_shape=jax.ShapeDtypeStruct((t, d), F32),
        scratch_shapes=[pltpu.VMEM((8, ncols), F32),
                        pltpu.VMEM((heads // 2, RWKV_HEAD, 2 * RWKV_HEAD), F32),
                        pltpu.VMEM((rows, W), F32)] + 2 * [
                        pltpu.VMEM((nchunks, 2 * CHUNK, W), BF16),
                        pltpu.VMEM((nchunks, W, 4 * CHUNK), BF16),
                        pltpu.VMEM((nchunks, RWKV_HEAD, 2 * W), BF16),
                        pltpu.VMEM((rows, W), F32),
                        pltpu.VMEM((rows, W), F32),
                        pltpu.VMEM((nchunks, 1, W), F32),
                        pltpu.VMEM((rows, W), F32),
                        pltpu.VMEM((rows, W), F32)],
        compiler_params=pltpu.CompilerParams(dimension_semantics=("arbitrary",),
                                             vmem_limit_bytes=VMEM_LIMIT),
        name="rwkv_mix",
    )(z_rwkv, *params)


def _s5_disc_kernel(are_ref, aim_ref, ldt_ref, bre_ref, bim_ref, abr_ref, abi_ref, bbr_ref, bbi_ref):
    dt = jnp.exp(ldt_ref[...])
    are = jnp.minimum(are_ref[...], -1e-4)
    aim = aim_ref[...]
    mag = jnp.exp(dt * are)
    abr = mag * jnp.cos(dt * aim)
    abi = mag * jnp.sin(dt * aim)
    den = are * are + aim * aim
    nr = abr - 1.0
    cre = (nr * are + abi * aim) / den
    cim = (abi * are - nr * aim) / den
    abr_ref[...] = abr
    abi_ref[...] = abi
    br = bre_ref[...]
    bi = bim_ref[...]
    bbr_ref[...] = cre[None] * br - cim[None] * bi
    bbi_ref[...] = cre[None] * bi + cim[None] * br


def _s5_disc(a_re, a_im, log_dt, b_re_c, b_im_c):
    G, P = a_re.shape
    C = b_re_c.shape[0]
    return pl.pallas_call(
        _s5_disc_kernel,
        out_shape=[jax.ShapeDtypeStruct((G, P), F32), jax.ShapeDtypeStruct((G, P), F32),
                   jax.ShapeDtypeStruct((C, G, P), F32), jax.ShapeDtypeStruct((C, G, P), F32)],
        name="s5_disc",
    )(a_re, a_im, log_dt.reshape(G, 1), b_re_c, b_im_c)


def _s5_kernel(u_ref, bmat_ref, cmat_ref, abr_ref, abi_ref, dskip_ref, wglu_ref, bglu_ref,
               out_ref, st_r_ref, st_i_ref, x_scr, bt_scr, tb_scr, *, nslab):
    _, nb, rt, _ = u_ref.shape
    rows = nb * rt
    ns = abr_ref.shape[1]
    sw = ns // nslab
    d = out_ref.shape[2]

    @pl.when(pl.program_id(1) == 0)
    def _():
        st_r_ref[...] = jnp.zeros_like(st_r_ref)
        st_i_ref[...] = jnp.zeros_like(st_i_ref)

    bt_scr[...] = u_ref[...].reshape(nslab, rows, 128)
    ys = []
    for s in range(nslab):
        for t in range(rt):
            tb_scr[s, t * nb:(t + 1) * nb, :] = bt_scr[s, pl.ds(t, nb, stride=rt), :]
        bu = _dotb(tb_scr[s], bmat_ref[s])
        cs = slice(s * sw, (s + 1) * sw)
        ar = abr_ref[:, cs]
        ai = abi_ref[:, cs]
        xr = st_r_ref[:, cs]
        xi = st_i_ref[:, cs]
        for t in range(rt):
            ts = slice(t * nb, (t + 1) * nb)
            xr, xi = ar * xr - ai * xi + bu[ts, :sw], ar * xi + ai * xr + bu[ts, sw:]
            x_scr[ts, :sw] = xr
            x_scr[ts, sw:] = xi
        st_r_ref[:, cs] = xr
        st_i_ref[:, cs] = xi
        tb_scr[s] = _dotb(x_scr[...], cmat_ref[s])
        y_s = jnp.concatenate([tb_scr[s, pl.ds(b, rt, stride=nb), :] for b in range(nb)], axis=0)
        ys.append(y_s + dskip_ref[:, s * 128:(s + 1) * 128] * bt_scr[s])
    y = jnp.concatenate(ys, axis=-1)
    y = 0.5 * y * (1.0 + lax.erf(y * (1.0 / math.sqrt(2.0))))
    zz = _dotb(y, wglu_ref[...]) + bglu_ref[...]
    out_ref[...] = (zz[:, :d] * jax.nn.sigmoid(zz[:, d:])).reshape(nb, rt, d)


def _s5_mix(u4, layer, bmat, cmat, abr, abi, dskip, wglu, bglu, nb=8, rt=64):
    nslab, bsz, seq, _ = u4.shape
    assert rt % 8 == 0 and bmat.shape[1] == nslab
    ns = abr.shape[-1]
    d = wglu.shape[-1] // 2
    full = lambda arr: _layer_spec(arr, layer)
    params = (bmat, cmat, abr, abi, dskip, wglu, bglu)
    return pl.pallas_call(
        functools.partial(_s5_kernel, nslab=nslab),
        grid=(bsz // nb, seq // rt),
        in_specs=[pl.BlockSpec((nslab, nb, rt, 128), lambda b, j: (0, b, j, 0))] + [full(p) for p in params],
        out_specs=pl.BlockSpec((nb, rt, d), lambda b, j: (b, j, 0)),
        out_shape=jax.ShapeDtypeStruct((bsz, seq, d), F32),
        scratch_shapes=[pltpu.VMEM((nb, ns), F32), pltpu.VMEM((nb, ns), F32),
                        pltpu.VMEM((nb * rt, 2 * ns // nslab), F32),
                        pltpu.VMEM((nslab, nb * rt, 128), F32),
                        pltpu.VMEM((nslab, nb * rt, 128), F32)],
        compiler_params=pltpu.CompilerParams(dimension_semantics=("parallel", "arbitrary"),
                                             vmem_limit_bytes=VMEM_LIMIT),
        name="s5_mix",
    )(u4, *params)


def _merge_ffn_kernel(x_ref, gt_ref, ya_ref, yb_ref, wout_ref, g2_ref, wup_ref, wdn_ref, gf_ref,
                      out_ref, *, final_norm, ff_chunk):
    d = x_ref.shape[1]
    gt = jax.nn.sigmoid(gt_ref[...])
    m = gt[:, :d] * ya_ref[...] + gt[:, d:] * yb_ref[...]
    x1 = x_ref[...] + _dotb(m, wout_ref[...])
    xn = x1 * lax.rsqrt(jnp.mean(x1 * x1, axis=-1, keepdims=True) + NORM_EPS) * g2_ref[...]
    xn = xn.astype(BF16)
    acc = x1
    dff = wup_ref.shape[1]
    for c in range(dff // ff_chunk):
        cs = slice(c * ff_chunk, (c + 1) * ff_chunk)
        h = jnp.dot(xn, wup_ref[:, cs], preferred_element_type=F32)
        h = jnp.square(jnp.maximum(h, 0.0))
        acc = acc + _dotb(h, wdn_ref[cs, :])
    if final_norm:
        acc = acc * lax.rsqrt(jnp.mean(acc * acc, axis=-1, keepdims=True) + NORM_EPS) * gf_ref[...]
    out_ref[...] = acc


def _merge_ffn(x2, gates, ya, yb, layer, wout, g2, wup, wdn, gf, final_norm, tm=512, ff_chunk=1024):
    t, d = x2.shape
    row = lambda w: pl.BlockSpec((tm, w), lambda i: (i, 0))
    full = lambda arr: _layer_spec(arr, layer)
    return pl.pallas_call(
        functools.partial(_merge_ffn_kernel, final_norm=final_norm, ff_chunk=ff_chunk),
        grid=(t // tm,),
        in_specs=[row(d), row(2 * d), row(d), row(d), full(wout), full(g2), full(wup), full(wdn),
                  pl.BlockSpec(gf.shape, lambda i: (0, 0))],
        out_specs=row(d),
        out_shape=jax.ShapeDtypeStruct((t, d), F32),
        compiler_params=pltpu.CompilerParams(dimension_semantics=("parallel",),
                                             vmem_limit_bytes=VMEM_LIMIT),
        name="merge_ffn",
    )(x2, gates, ya, yb, wout, g2, wup, wdn, gf)


def _s5_layer_params(a_re, a_im, log_dt, b_re, b_im, c_re, c_im):
    G, P, C = b_re.shape
    gps = 128 // C
    nslab = G // gps
    abr, abi, bbr, bbi = _s5_disc(a_re, a_im, log_dt, jnp.transpose(b_re, (2, 0, 1)),
                                  jnp.transpose(b_im, (2, 0, 1)))
    eye = jnp.eye(gps, dtype=F32)

    def bd_in(bb):
        bb = jnp.transpose(bb, (1, 0, 2)).reshape(nslab, gps, C, P)
        return jnp.einsum('sgcp,gh->sgchp', bb, eye).reshape(nslab, gps * C, gps * P)

    def bd_out(cc):
        cc = cc.reshape(nslab, gps, C, P)
        return jnp.einsum('sgcp,gh->sgphc', cc, eye).reshape(nslab, gps * P, gps * C)

    bmat = jnp.concatenate([bd_in(bbr), bd_in(bbi)], axis=2).astype(BF16)
    cmat = jnp.concatenate([bd_out(c_re), -bd_out(c_im)], axis=1).astype(BF16)
    return bmat, cmat, abr.reshape(1, G * P), abi.reshape(1, G * P)


def kernel(x, norm1_g, w_in, mu_shift, k_k, k_a, r_k, w0, w_decay_up, a0, w_aaa_up, w_gate_up,
           lnx_w, lnx_b, w_rwkv_proj, a_re, a_im, log_dt, b_re, b_im, c_re, c_im, d_skip, w_glu,
           b_glu, w_out, norm2_g, w_ff_up, w_ff_down, norm_f_g):
    bsz, seq, d = x.shape
    depth = w_in.shape[0]
    n_shift = mu_shift.shape[1]
    n_ssm = d_skip.shape[1]
    t = bsz * seq
    vec = lambda a: a.reshape(a.shape[0], 1, -1)
    bf = lambda a: a.astype(BF16)
    rwkv_params = (vec(mu_shift), vec(k_k), vec(k_a), vec(r_k), vec(w0), bf(w_decay_up), vec(a0),
                   bf(w_aaa_up), bf(w_gate_up), vec(lnx_w), vec(lnx_b), bf(w_rwkv_proj))
    s5_params = jax.vmap(_s5_layer_params)(a_re, a_im, log_dt, b_re, b_im, c_re, c_im)
    s5_params += (vec(d_skip), bf(w_glu), vec(b_glu))
    w_in_bf, w_out_bf, w_up_bf, w_dn_bf = bf(w_in), bf(w_out), bf(w_ff_up), bf(w_ff_down)
    g1, g2 = vec(norm1_g), vec(norm2_g)
    x2 = x.reshape(t, d)
    for l in range(depth):
        z_rwkv, u, gates = _in_proj(x2, g1, w_in_bf, l, n_shift, n_ssm)
        y_a = _rwkv_mix(z_rwkv, seq, l, *rwkv_params)
        y_b = _s5_mix(u.reshape(-1, bsz, seq, 128), l, *s5_params)
        x2 = _merge_ffn(x2, gates, y_a, y_b.reshape(t, d), l, w_out_bf, g2, w_up_bf,
                        w_dn_bf, norm_f_g.reshape(1, -1), final_norm=(l == depth - 1))
    return x2.reshape(bsz, seq, d)
```

```python
import functools
import math

import jax
import jax.numpy as jnp
from jax import lax
from jax.experimental import pallas as pl
from jax.experimental.pallas import tpu as pltpu

F32 = jnp.float32
BF16 = jnp.bfloat16

RWKV_HEAD = 64
DECAY_LORA = 64
AAA_LORA = 64
LNX_EPS = 64e-5
DECAY_SCALE = math.exp(-0.5)
NORM_EPS = 1e-6

CHUNK = 64
VMEM_LIMIT = 56 * 1024 * 1024


def _dotb(a, b):
    return jnp.dot(a.astype(BF16), b.astype(BF16), preferred_element_type=F32)


def _iota2(shape, axis):
    return lax.broadcasted_iota(jnp.int32, shape, axis)


def _layer_spec(arr, layer):
    return pl.BlockSpec((None,) + arr.shape[1:], lambda *_: (layer,) + (0,) * (arr.ndim - 1),
                        pipeline_mode=pl.Buffered(1))


def _in_proj_kernel(x_ref, g_ref, w_ref, zr_ref, u_ref, gt_ref, *, n_shift, n_ssm):
    x = x_ref[...]
    xn = x * lax.rsqrt(jnp.mean(x * x, axis=-1, keepdims=True) + NORM_EPS) * g_ref[...]
    z = _dotb(xn, w_ref[...])
    zr_ref[...] = z[:, :n_shift]
    for s in range(u_ref.shape[0]):
        u_ref[s] = z[:, n_shift + s * 128:n_shift + (s + 1) * 128]
    gt_ref[...] = z[:, n_shift + n_ssm:]


def _in_proj(x2, g, w_bf, layer, n_shift, n_ssm, tm=512):
    t, d = x2.shape
    n_in = w_bf.shape[-1]
    n_gate = n_in - n_shift - n_ssm
    return pl.pallas_call(
        functools.partial(_in_proj_kernel, n_shift=n_shift, n_ssm=n_ssm),
        grid=(t // tm,),
        in_specs=[pl.BlockSpec((tm, d), lambda i: (i, 0)),
                  _layer_spec(g, layer), _layer_spec(w_bf, layer)],
        out_specs=[pl.BlockSpec((tm, n_shift), lambda i: (i, 0)),
                   pl.BlockSpec((n_ssm // 128, tm, 128), lambda i: (0, i, 0)),
                   pl.BlockSpec((tm, n_gate), lambda i: (i, 0))],
        out_shape=[jax.ShapeDtypeStruct((t, n_shift), F32),
                   jax.ShapeDtypeStruct((n_ssm // 128, t, 128), F32),
                   jax.ShapeDtypeStruct((t, n_gate), F32)],
        compiler_params=pltpu.CompilerParams(dimension_semantics=("parallel",),
                                             vmem_limit_bytes=VMEM_LIMIT),
        name="in_proj",
    )(x2, g, w_bf)


def _dot_exact(m_bf, x, terms, *, m_left):
    acc = None
    rem = x
    for _ in range(terms):
        piece = rem.astype(BF16)
        d = (jnp.dot(m_bf, piece, preferred_element_type=F32) if m_left
             else jnp.dot(piece, m_bf, preferred_element_type=F32))
        acc = d if acc is None else acc + d
        rem = rem - piece.astype(F32)
    return acc


def _segment_sums(ts, n):
    rows, width = ts[0].shape
    tile = 256
    ncol = width // tile
    lg = n.bit_length() - 1
    seg = jnp.where((_iota2((tile, tile), 0) >> lg) == (_iota2((tile, tile), 1) >> lg), 1.0, 0.0).astype(BF16)
    pieces = []
    for t in ts:
        hi = t.astype(BF16)
        lo = (t - hi.astype(F32)).astype(BF16)
        pieces += [part[:, j * tile:(j + 1) * tile] for part in (hi, lo) for j in range(ncol)]
    res = jnp.dot(jnp.concatenate(pieces, axis=0), seg, preferred_element_type=F32)
    blk = lambda i: res[i * rows:(i + 1) * rows]
    return [jnp.concatenate([blk((2 * a) * ncol + j) + blk((2 * a + 1) * ncol + j) for j in range(ncol)], axis=1)
            for a in range(len(ts))]


def _pair_blockdiag(m_cat, mask=None):
    m2 = jnp.concatenate([m_cat, m_cat], axis=0)
    keep = (_iota2(m2.shape, 0) >> 6) == (_iota2(m2.shape, 1) >> 6)
    if mask is not None:
        keep = keep & jnp.concatenate([mask, mask], axis=0)
    return jnp.where(keep, m2, jnp.zeros_like(m2)).astype(BF16)


def _pair_dot(a_cat, b_cat=None, b_bd=None, mask=None):
    b_bd = _pair_blockdiag(b_cat, mask) if b_bd is None else b_bd
    return jnp.dot(a_cat.astype(BF16), b_bd, preferred_element_type=F32)


def _unit_lower_inverse(ns, between=lambda: None):
    L, lanes = ns[0].shape
    ri = _iota2((L, lanes), 0)
    ci = _iota2((L, lanes), 1) & (L - 1)
    eye = (ri == ci).astype(F32)
    same8 = ((ri >> 3) == (ci >> 3)) & (ri > ci)
    n0 = [jnp.where(same8, m, 0.0) for m in ns]
    x = [eye - m for m in n0]
    p = [_pair_dot(m0, m, mask=same8) for m0, m in zip(n0, ns)]
    between()
    pbd = [_pair_blockdiag(pi) for pi in p]
    x = [xi + _pair_dot(xi, b_bd=pi) for xi, pi in zip(x, pbd)]
    between()
    p = [_pair_dot(pi, b_bd=qi) for pi, qi in zip(p, pbd)]
    between()
    x = [xi + _pair_dot(xi, pi) for xi, pi in zip(x, p)]
    between()
    shift = 4
    while (1 << shift) <= L:
        sel = ((ri >> shift) == (ci >> shift)) & ((ri >> (shift - 1)) > (ci >> (shift - 1)))
        xc = [_pair_dot(xi, m, mask=sel).astype(BF16) for xi, m in zip(x, ns)]
        between()
        x = [xi - _pair_dot(xci, xi) for xi, xci in zip(x, xc)]
        between()
        shift += 1
    return x


def _rwkv_kernel(z_ref, mu_ref, kk_ref, ka_ref, rk_ref, w0_ref, wdu_ref, a0_ref, wau_ref,
                 wgu_ref, lnw_ref, lnb_ref, wproj_ref, out_ref, carry_ref, state_ref, y_scr,
                 *staging, width, nblk):
    step = pl.program_id(0)
    nstage = len(staging) // 2

    @pl.when(step == 0)
    def _():
        for ref in (carry_ref, state_ref) + tuple(staging):
            ref[...] = jnp.zeros_like(ref)

    for parity in range(2):
        @pl.when((step & 1) == parity)
        def _(parity=parity):
            _rwkv_step(z_ref, mu_ref, kk_ref, ka_ref, rk_ref, w0_ref, wdu_ref, a0_ref, wau_ref, wgu_ref,
                       lnw_ref, lnb_ref, wproj_ref, out_ref, carry_ref, state_ref, y_scr,
                       staging[parity * nstage:(parity + 1) * nstage],
                       staging[(1 - parity) * nstage:(2 - parity) * nstage], step, width, nblk)


def _rwkv_step(z_ref, mu_ref, kk_ref, ka_ref, rk_ref, w0_ref, wdu_ref, a0_ref, wau_ref, wgu_ref,
               lnw_ref, lnb_ref, wproj_ref, out_ref, carry_ref, state_ref, y_scr, cur, prev, step, width, nblk):
    lhs1_w, rhs1_w, lhs2_w, rt_w, v_w, glast_w, g_w, bonus_w = cur
    lhs1_s, rhs1_s, lhs2_s, rt_s, v_s, glast_s, g_s, bonus_s = prev
    rows = z_ref.shape[0]
    n = RWKV_HEAD
    L = CHUNK
    nchunks = rows // L
    W = width
    PW = 2 * n
    npairs = W // PW
    psl = lambda p: slice(p * PW, (p + 1) * PW)
    csl = lambda c: slice(c * L, (c + 1) * L)

    t = {}

    seq_start = lax.rem(step, nblk) == 0

    def shift_mix(lo, hi):
        z = z_ref[:, lo:hi]
        row_id = _iota2((rows, 1), 0)
        carry = jnp.where(seq_start, 0.0, carry_ref[0:1, lo:hi])
        zprev = jnp.where(row_id == 0, carry, pltpu.roll(z, 1, axis=0))
        carry_ref[0:1, lo:hi] = z[rows - 1:rows, :]
        return z + (zprev - z) * mu_ref[:, lo:hi]

    def task_r():
        t['r'] = shift_mix(0, W)

    def task_k():
        t['k'] = shift_mix(W, 2 * W)

    def task_v():
        t['v'] = shift_mix(2 * W, 3 * W)
        v_w[...] = t['v']

    def task_lora():
        zs = shift_mix(3 * W, z_ref.shape[1])
        wd = zs[:, :DECAY_LORA]
        ad = zs[:, DECAY_LORA:DECAY_LORA + AAA_LORA]
        gd = zs[:, DECAY_LORA + AAA_LORA:]
        t['lw'] = -DECAY_SCALE * jax.nn.sigmoid(w0_ref[...] + _dotb(jnp.tanh(wd), wdu_ref[...]))
        t['a'] = jax.nn.sigmoid(a0_ref[...] + _dotb(ad, wau_ref[...]))
        g_w[...] = _dotb(jax.nn.sigmoid(gd), wgu_ref[...])

    def task_kk():
        t['kk'] = t['k'] * kk_ref[...]
        t['k_mod'] = t['k'] * (1.0 + (t['a'] - 1.0) * ka_ref[...])

    def task_norm():
        kk_sq, rk_sum = _segment_sums([t['kk'] * t['kk'], t['r'] * t['k_mod'] * rk_ref[...]], n)
        t['kk'] = t['kk'] / jnp.maximum(jnp.sqrt(kk_sq), 1e-12)
        t['b'] = t['kk'] * t['a']
        bonus_w[...] = rk_sum * t['v']

    tri_bf = jnp.where(_iota2((L, L), 0) >= _iota2((L, L), 1), 1.0, 0.0).astype(BF16)

    def task_decay(c):
        def run():
            sl = csl(c)
            lw_c = t['lw'][sl]
            cum = _dot_exact(tri_bf, lw_c, 3, m_left=True)
            cum_last = cum[L - 1:L, :]
            t['g_inv', c] = jnp.exp(-cum)
            t['g_end', c] = jnp.exp(cum_last - cum)
            glast_w[c] = jnp.exp(cum_last)
            rt = t['r'][sl] * jnp.exp(cum)
            at = t['kk'][sl] * jnp.exp(cum - lw_c)
            rt_w[sl] = rt
            lhs1_w[c] = jnp.concatenate([at, rt], axis=0).astype(BF16)
        return run

    def dup_t(t1, t2):
        return jnp.concatenate([t1, t1, t2, t2], axis=0).T

    copy_is_head = (_iota2((W, 4 * L), 0) >> 6 & 1) == (_iota2((W, 4 * L), 1) >> 6 & 1)

    def task_rhs1(c):
        def run():
            sl = csl(c)
            bk = dup_t(t['b'][sl] * t['g_inv', c], t['k_mod'][sl] * t['g_inv', c]).astype(BF16)
            rhs1_w[c] = jnp.where(copy_is_head, bk, jnp.zeros_like(bk))
        return run

    first_copy = (_iota2((n, 2 * PW), 1) >> 6 & 1) == 0

    def task_lhs2(c):
        def run():
            sl = csl(c)
            bk = dup_t(t['b'][sl] * t['g_end', c], t['k_mod'][sl] * t['g_end', c])
            for p in range(npairs):
                lhs2_w[c, :, p * 2 * PW:(p + 1) * 2 * PW] = jnp.where(
                    first_copy, bk[p * PW:p * PW + n], bk[p * PW + n:(p + 1) * PW]).astype(BF16)
        return run

    tasks = [task_r, task_k, task_v, task_lora, task_kk, task_norm]
    for c in range(nchunks):
        tasks += [task_decay(c), task_rhs1(c), task_lhs2(c)]
    tasks = iter(tasks)

    def tick(k=1):
        for _ in range(k):
            task = next(tasks, None)
            if task is not None:
                task()

    ri = _iota2((L, 2 * PW), 0)
    ci = _iota2((L, 2 * PW), 1) & (L - 1)
    strict = (ri > ci)[:, :PW]
    incl = ri >= ci
    eye_cat = (_iota2((n, PW), 0) == (_iota2((n, PW), 1) & (n - 1))).astype(F32)
    zeros_bd = jnp.zeros((PW, PW), BF16)
    units = [(c, p) for c in range(nchunks) for p in range(npairs)]
    x1 = [jnp.dot(lhs1_s[c, :, psl(p)], rhs1_s[c, psl(p), :], preferred_element_type=F32)
          for c, p in units]
    n_ab = [m[:L, :PW] for m in x1]
    n_ak = [jnp.where(strict, m[:L, PW:], 0.0).astype(BF16) for m in x1]
    m_rbk = [jnp.where(incl, m[L:, :], 0.0).astype(BF16) for m in x1]
    tick()
    t_inv = _unit_lower_inverse(n_ab, between=tick)
    vbd = [_pair_blockdiag(v_s[csl(c), psl(p)]) for c, p in units]
    nv = [_pair_dot(a, b_bd=vb) for a, vb in zip(n_ak, vbd)]
    tick()
    pq = [jnp.dot(tm.astype(BF16),
                  jnp.concatenate([_pair_blockdiag(lhs1_s[c, :L, psl(p)]), _pair_blockdiag(-q)], axis=1),
                  preferred_element_type=F32) for tm, q, (c, p) in zip(t_inv, nv, units)]
    tick()
    r2 = []
    for m, pqi, vb, (c, p) in zip(m_rbk, pq, vbd, units):
        rhs2 = jnp.concatenate(
            [jnp.concatenate([_pair_blockdiag(pqi[:, :PW]), _pair_blockdiag(pqi[:, PW:])], axis=1),
             jnp.concatenate([zeros_bd, vb], axis=1)], axis=0)
        l2 = jnp.concatenate([m, lhs2_s[c, :, p * 2 * PW:(p + 1) * 2 * PW]], axis=0)
        r2.append(jnp.dot(l2, rhs2, preferred_element_type=F32))
    tick()
    lhs3 = [jnp.concatenate([rt_s[csl(c), psl(p)] - m[:L, :PW],
                             eye_cat * glast_s[c][:, psl(p)] - m[L:, :PW]], axis=0)
            for m, (c, p) in zip(r2, units)]

    first = lax.rem(step + nblk - 1, nblk) == 0
    state = [jnp.where(first, 0.0, state_ref[p]) for p in range(npairs)]
    for c in range(nchunks):
        r3 = [_pair_dot(lhs3[c * npairs + p], state[p]) for p in range(npairs)]
        for p in range(npairs):
            m = r2[c * npairs + p]
            y_scr[csl(c), psl(p)] = r3[p][:L] + m[:L, PW:]
            state[p] = r3[p][L:] + m[L:, PW:]
        tick()
    for p in range(npairs):
        state_ref[p] = state[p]
    tick(len(units))

    y = y_scr[...]
    inv_n = 1.0 / n
    yc = y - _segment_sums([y], n)[0] * inv_n
    var = _segment_sums([yc * yc], n)[0] * inv_n
    yn = yc * lax.rsqrt(var + LNX_EPS) * lnw_ref[...] + lnb_ref[...]
    out_ref[...] = _dotb((yn + bonus_s[...]) * g_s[...], wproj_ref[...])


def _rwkv_mix(z_rwkv, seq, layer, mu, k_k, k_a, r_k, w0, wdu, a0, wau, wgu, lnw, lnb, wproj, rows=256):
    t, ncols = z_rwkv.shape
    W = k_k.shape[-1]
    heads = W // RWKV_HEAD
    d = wproj.shape[-1]
    nblk = seq // rows
    nsteps = t // rows
    nchunks = rows // CHUNK
    full = lambda arr: _layer_spec(arr, layer)
    params = (mu, k_k, k_a, r_k, w0, wdu, a0, wau, wgu, lnw, lnb, wproj)
    return pl.pallas_call(
        functools.partial(_rwkv_kernel, width=W, nblk=nblk),
        grid=(nsteps + 1,),
        in_specs=[pl.BlockSpec((rows, ncols), lambda s: (jnp.minimum(s, nsteps - 1), 0))]
        + [full(p) for p in params],
        out_specs=pl.BlockSpec((rows, d), lambda s: (jnp.maximum(s - 1, 0), 0)),
        out_shape=jax.ShapeDtypeStruct((t, d), F32),
        scratch_shapes=[pltpu.VMEM((8, ncols), F32),
                        pltpu.VMEM((heads // 2, RWKV_HEAD, 2 * RWKV_HEAD), F32),
                        pltpu.VMEM((rows, W), F32)] + 2 * [
                        pltpu.VMEM((nchunks, 2 * CHUNK, W), BF16),
                        pltpu.VMEM((nchunks, W, 4 * CHUNK), BF16),
                        pltpu.VMEM((nchunks, RWKV_HEAD, 2 * W), BF16),
                        pltpu.VMEM((rows, W), F32),
                        pltpu.VMEM((rows, W), F32),
                        pltpu.VMEM((nchunks, 1, W), F32),
                        pltpu.VMEM((rows, W), F32),
                        pltpu.VMEM((rows, W), F32)],
        compiler_params=pltpu.CompilerParams(dimension_semantics=("arbitrary",),
                                             vmem_limit_bytes=VMEM_LIMIT),
        name="rwkv_mix",
    )(z_rwkv, *params)


def _s5_disc_kernel(are_ref, aim_ref, ldt_ref, bre_ref, bim_ref, abr_ref, abi_ref, bbr_ref, bbi_ref):
    dt = jnp.exp(ldt_ref[...])
    are = jnp.minimum(are_ref[...], -1e-4)
    aim = aim_ref[...]
    mag = jnp.exp(dt * are)
    abr = mag * jnp.cos(dt * aim)
    abi = mag * jnp.sin(dt * aim)
    den = are * are + aim * aim
    nr = abr - 1.0
    cre = (nr * are + abi * aim) / den
    cim = (abi * are - nr * aim) / den
    abr_ref[...] = abr
    abi_ref[...] = abi
    br = bre_ref[...]
    bi = bim_ref[...]
    bbr_ref[...] = cre[None] * br - cim[None] * bi
    bbi_ref[...] = cre[None] * bi + cim[None] * br


def _s5_disc(a_re, a_im, log_dt, b_re_c, b_im_c):
    G, P = a_re.shape
    C = b_re_c.shape[0]
    return pl.pallas_call(
        _s5_disc_kernel,
        out_shape=[jax.ShapeDtypeStruct((G, P), F32), jax.ShapeDtypeStruct((G, P), F32),
                   jax.ShapeDtypeStruct((C, G, P), F32), jax.ShapeDtypeStruct((C, G, P), F32)],
        name="s5_disc",
    )(a_re, a_im, log_dt.reshape(G, 1), b_re_c, b_im_c)


def _s5_kernel(u_ref, bmat_ref, cmat_ref, abr_ref, abi_ref, dskip_ref, wglu_ref, bglu_ref,
               out_ref, st_r_ref, st_i_ref, x_scr, bt_scr, tb_scr, *, nslab):
    _, nb, rt, _ = u_ref.shape
    rows = nb * rt
    ns = abr_ref.shape[1]
    sw = ns // nslab
    d = out_ref.shape[2]

    @pl.when(pl.program_id(1) == 0)
    def _():
        st_r_ref[...] = jnp.zeros_like(st_r_ref)
        st_i_ref[...] = jnp.zeros_like(st_i_ref)

    bt_scr[...] = u_ref[...].reshape(nslab, rows, 128)
    ys = []
    for s in range(nslab):
        for t in range(rt):
            tb_scr[s, t * nb:(t + 1) * nb, :] = bt_scr[s, pl.ds(t, nb, stride=rt), :]
        bu = _dotb(tb_scr[s], bmat_ref[s])
        cs = slice(s * sw, (s + 1) * sw)
        ar = abr_ref[:, cs]
        ai = abi_ref[:, cs]
        xr = st_r_ref[:, cs]
        xi = st_i_ref[:, cs]
        for t in range(rt):
            ts = slice(t * nb, (t + 1) * nb)
            xr, xi = ar * xr - ai * xi + bu[ts, :sw], ar * xi + ai * xr + bu[ts, sw:]
            x_scr[ts, :sw] = xr
            x_scr[ts, sw:] = xi
        st_r_ref[:, cs] = xr
        st_i_ref[:, cs] = xi
        tb_scr[s] = _dotb(x_scr[...], cmat_ref[s])
        y_s = jnp.concatenate([tb_scr[s, pl.ds(b, rt, stride=nb), :] for b in range(nb)], axis=0)
        ys.append(y_s + dskip_ref[:, s * 128:(s + 1) * 128] * bt_scr[s])
    y = jnp.concatenate(ys, axis=-1)
    y = 0.5 * y * (1.0 + lax.erf(y * (1.0 / math.sqrt(2.0))))
    zz = _dotb(y, wglu_ref[...]) + bglu_ref[...]
    out_ref[...] = (zz[:, :d] * jax.nn.sigmoid(zz[:, d:])).reshape(nb, rt, d)


def _s5_mix(u4, layer, bmat, cmat, abr, abi, dskip, wglu, bglu, nb=8, rt=64):
    nslab, bsz, seq, _ = u4.shape
    assert rt % 8 == 0 and bmat.shape[1] == nslab
    ns = abr.shape[-1]
    d = wglu.shape[-1] // 2
    full = lambda arr: _layer_spec(arr, layer)
    params = (bmat, cmat, abr, abi, dskip, wglu, bglu)
    return pl.pallas_call(
        functools.partial(_s5_kernel, nslab=nslab),
        grid=(bsz // nb, seq // rt),
        in_specs=[pl.BlockSpec((nslab, nb, rt, 128), lambda b, j: (0, b, j, 0))] + [full(p) for p in params],
        out_specs=pl.BlockSpec((nb, rt, d), lambda b, j: (b, j, 0)),
        out_shape=jax.ShapeDtypeStruct((bsz, seq, d), F32),
        scratch_shapes=[pltpu.VMEM((nb, ns), F32), pltpu.VMEM((nb, ns), F32),
                        pltpu.VMEM((nb * rt, 2 * ns // nslab), F32),
                        pltpu.VMEM((nslab, nb * rt, 128), F32),
                        pltpu.VMEM((nslab, nb * rt, 128), F32)],
        compiler_params=pltpu.CompilerParams(dimension_semantics=("parallel", "arbitrary"),
                                             vmem_limit_bytes=VMEM_LIMIT),
        name="s5_mix",
    )(u4, *params)


def _merge_ffn_kernel(x_ref, gt_ref, ya_ref, yb_ref, wout_ref, g2_ref, wup_ref, wdn_ref, gf_ref,
                      out_ref, *, final_norm, ff_chunk):
    d = x_ref.shape[1]
    gt = jax.nn.sigmoid(gt_ref[...])
    m = gt[:, :d] * ya_ref[...] + gt[:, d:] * yb_ref[...]
    x1 = x_ref[...] + _dotb(m, wout_ref[...])
    xn = x1 * lax.rsqrt(jnp.mean(x1 * x1, axis=-1, keepdims=True) + NORM_EPS) * g2_ref[...]
    xn = xn.astype(BF16)
    acc = x1
    dff = wup_ref.shape[1]
    for c in range(dff // ff_chunk):
        cs = slice(c * ff_chunk, (c + 1) * ff_chunk)
        h = jnp.dot(xn, wup_ref[:, cs], preferred_element_type=F32)
        h = jnp.square(jnp.maximum(h, 0.0))
        acc = acc + _dotb(h, wdn_ref[cs, :])
    if final_norm:
        acc = acc * lax.rsqrt(jnp.mean(acc * acc, axis=-1, keepdims=True) + NORM_EPS) * gf_ref[...]
    out_ref[...] = acc


def _merge_ffn(x2, gates, ya, yb, layer, wout, g2, wup, wdn, gf, final_norm, tm=512, ff_chunk=1024):
    t, d = x2.shape
    row = lambda w: pl.BlockSpec((tm, w), lambda i: (i, 0))
    full = lambda arr: _layer_spec(arr, layer)
    return pl.pallas_call(
        functools.partial(_merge_ffn_kernel, final_norm=final_norm, ff_chunk=ff_chunk),
        grid=(t // tm,),
        in_specs=[row(d), row(2 * d), row(d), row(d), full(wout), full(g2), full(wup), full(wdn),
                  pl.BlockSpec(gf.shape, lambda i: (0, 0))],
        out_specs=row(d),
        out_shape=jax.ShapeDtypeStruct((t, d), F32),
        compiler_params=pltpu.CompilerParams(dimension_semantics=("parallel",),
                                             vmem_limit_bytes=VMEM_LIMIT),
        name="merge_ffn",
    )(x2, gates, ya, yb, wout, g2, wup, wdn, gf)


def _s5_layer_params(a_re, a_im, log_dt, b_re, b_im, c_re, c_im):
    G, P, C = b_re.shape
    gps = 128 // C
    nslab = G // gps
    abr, abi, bbr, bbi = _s5_disc(a_re, a_im, log_dt, jnp.transpose(b_re, (2, 0, 1)),
                                  jnp.transpose(b_im, (2, 0, 1)))
    eye = jnp.eye(gps, dtype=F32)

    def bd_in(bb):
        bb = jnp.transpose(bb, (1, 0, 2)).reshape(nslab, gps, C, P)
        return jnp.einsum('sgcp,gh->sgchp', bb, eye).reshape(nslab, gps * C, gps * P)

    def bd_out(cc):
        cc = cc.reshape(nslab, gps, C, P)
        return jnp.einsum('sgcp,gh->sgphc', cc, eye).reshape(nslab, gps * P, gps * C)

    bmat = jnp.concatenate([bd_in(bbr), bd_in(bbi)], axis=2).astype(BF16)
    cmat = jnp.concatenate([bd_out(c_re), -bd_out(c_im)], axis=1).astype(BF16)
    return bmat, cmat, abr.reshape(1, G * P), abi.reshape(1, G * P)


def kernel(x, norm1_g, w_in, mu_shift, k_k, k_a, r_k, w0, w_decay_up, a0, w_aaa_up, w_gate_up,
           lnx_w, lnx_b, w_rwkv_proj, a_re, a_im, log_dt, b_re, b_im, c_re, c_im, d_skip, w_glu,
           b_glu, w_out, norm2_g, w_ff_up, w_ff_down, norm_f_g):
    bsz, seq, d = x.shape
    depth = w_in.shape[0]
    n_shift = mu_shift.shape[1]
    n_ssm = d_skip.shape[1]
    t = bsz * seq
    vec = lambda a: a.reshape(a.shape[0], 1, -1)
    bf = lambda a: a.astype(BF16)
    rwkv_params = (vec(mu_shift), vec(k_k), vec(k_a), vec(r_k), vec(w0), bf(w_decay_up), vec(a0),
                   bf(w_aaa_up), bf(w_gate_up), vec(lnx_w), vec(lnx_b), bf(w_rwkv_proj))
    s5_params = jax.vmap(_s5_layer_params)(a_re, a_im, log_dt, b_re, b_im, c_re, c_im)
    s5_params += (vec(d_skip), bf(w_glu), vec(b_glu))
    w_in_bf, w_out_bf, w_up_bf, w_dn_bf = bf(w_in), bf(w_out), bf(w_ff_up), bf(w_ff_down)
    g1, g2 = vec(norm1_g), vec(norm2_g)
    x2 = x.reshape(t, d)
    for l in range(depth):
        z_rwkv, u, gates = _in_proj(x2, g1, w_in_bf, l, n_shift, n_ssm)
        y_a = _rwkv_mix(z_rwkv, seq, l, *rwkv_params)
        y_b = _s5_mix(u.reshape(-1, bsz, seq, 128), l, *s5_params)
        x2 = _merge_ffn(x2, gates, y_a, y_b.reshape(t, d), l, w_out_bf, g2, w_up_bf,
                        w_dn_bf, norm_f_g.reshape(1, -1), final_norm=(l == depth - 1))
    return x2.reshape(bsz, seq, d)
```

```python
import functools
import math

import jax
import jax.numpy as jnp
from jax import lax
from jax.experimental import pallas as pl
from jax.experimental.pallas import tpu as pltpu

F32 = jnp.float32
BF16 = jnp.bfloat16

RWKV_HEAD = 64
DECAY_LORA = 64
AAA_LORA = 64
LNX_EPS = 64e-5
DECAY_SCALE = math.exp(-0.5)
NORM_EPS = 1e-6

CHUNK = 64
VMEM_LIMIT = 56 * 1024 * 1024


def _dotb(a, b):
    return jnp.dot(a.astype(BF16), b.astype(BF16), preferred_element_type=F32)


def _sigmoid(x):
    return 0.5 * jnp.tanh(0.5 * x) + 0.5


def _iota2(shape, axis):
    return lax.broadcasted_iota(jnp.int32, shape, axis)


def _layer_spec(arr, layer):
    return pl.BlockSpec((None,) + arr.shape[1:], lambda *_: (layer,) + (0,) * (arr.ndim - 1),
                        pipeline_mode=pl.Buffered(1))


def _in_proj_kernel(x_ref, g_ref, w_ref, zr_ref, u_ref, gt_ref, *, n_shift, n_ssm):
    x = x_ref[...]
    xn = x * lax.rsqrt(jnp.mean(x * x, axis=-1, keepdims=True) + NORM_EPS) * g_ref[...]
    z = _dotb(xn, w_ref[...])
    zr_ref[...] = z[:, :n_shift]
    for s in range(u_ref.shape[0]):
        u_ref[s] = z[:, n_shift + s * 128:n_shift + (s + 1) * 128]
    gt_ref[...] = z[:, n_shift + n_ssm:]


def _in_proj(x2, g, w_bf, layer, n_shift, n_ssm, tm=512):
    t, d = x2.shape
    n_in = w_bf.shape[-1]
    n_gate = n_in - n_shift - n_ssm
    return pl.pallas_call(
        functools.partial(_in_proj_kernel, n_shift=n_shift, n_ssm=n_ssm),
        grid=(t // tm,),
        in_specs=[pl.BlockSpec((tm, d), lambda i: (i, 0)),
                  _layer_spec(g, layer), _layer_spec(w_bf, layer)],
        out_specs=[pl.BlockSpec((tm, n_shift), lambda i: (i, 0)),
                   pl.BlockSpec((n_ssm // 128, tm, 128), lambda i: (0, i, 0)),
                   pl.BlockSpec((tm, n_gate), lambda i: (i, 0))],
        out_shape=[jax.ShapeDtypeStruct((t, n_shift), F32),
                   jax.ShapeDtypeStruct((n_ssm // 128, t, 128), F32),
                   jax.ShapeDtypeStruct((t, n_gate), F32)],
        compiler_params=pltpu.CompilerParams(dimension_semantics=("parallel",),
                                             vmem_limit_bytes=VMEM_LIMIT),
        name="in_proj",
    )(x2, g, w_bf)


def _dot_exact(m_bf, x, terms, *, m_left):
    acc = None
    rem = x
    for _ in range(terms):
        piece = rem.astype(BF16)
        d = (jnp.dot(m_bf, piece, preferred_element_type=F32) if m_left
             else jnp.dot(piece, m_bf, preferred_element_type=F32))
        acc = d if acc is None else acc + d
        rem = rem - piece.astype(F32)
    return acc


def _segment_sums(ts, n):
    rows, width = ts[0].shape
    tile = 256
    ncol = width // tile
    lg = n.bit_length() - 1
    seg = jnp.where((_iota2((tile, tile), 0) >> lg) == (_iota2((tile, tile), 1) >> lg), 1.0, 0.0).astype(BF16)
    pieces = []
    for t in ts:
        hi = t.astype(BF16)
        lo = (t - hi.astype(F32)).astype(BF16)
        pieces += [part[:, j * tile:(j + 1) * tile] for part in (hi, lo) for j in range(ncol)]
    res = jnp.dot(jnp.concatenate(pieces, axis=0), seg, preferred_element_type=F32)
    blk = lambda i: res[i * rows:(i + 1) * rows]
    return [jnp.concatenate([blk((2 * a) * ncol + j) + blk((2 * a + 1) * ncol + j) for j in range(ncol)], axis=1)
            for a in range(len(ts))]


def _pair_blockdiag(m_cat, mask=None):
    m2 = jnp.concatenate([m_cat, m_cat], axis=0)
    keep = (_iota2(m2.shape, 0) >> 6) == (_iota2(m2.shape, 1) >> 6)
    if mask is not None:
        keep = keep & jnp.concatenate([mask, mask], axis=0)
    return jnp.where(keep, m2, jnp.zeros_like(m2)).astype(BF16)


def _pair_dot(a_cat, b_cat=None, b_bd=None, mask=None):
    b_bd = _pair_blockdiag(b_cat, mask) if b_bd is None else b_bd
    return jnp.dot(a_cat.astype(BF16), b_bd, preferred_element_type=F32)


def _unit_lower_inverse(ns, between=lambda: None):
    L, lanes = ns[0].shape
    ri = _iota2((L, lanes), 0)
    ci = _iota2((L, lanes), 1) & (L - 1)
    eye = (ri == ci).astype(F32)
    same8 = ((ri >> 3) == (ci >> 3)) & (ri > ci)
    n0 = [jnp.where(same8, m, 0.0) for m in ns]
    x = [eye - m for m in n0]
    p = [_pair_dot(m0, m, mask=same8) for m0, m in zip(n0, ns)]
    between()
    pbd = [_pair_blockdiag(pi) for pi in p]
    x = [xi + _pair_dot(xi, b_bd=pi) for xi, pi in zip(x, pbd)]
    between()
    p = [_pair_dot(pi, b_bd=qi) for pi, qi in zip(p, pbd)]
    between()
    x = [xi + _pair_dot(xi, pi) for xi, pi in zip(x, p)]
    between()
    shift = 4
    while (1 << shift) <= L:
        sel = ((ri >> shift) == (ci >> shift)) & ((ri >> (shift - 1)) > (ci >> (shift - 1)))
        xc = [_pair_dot(xi, m, mask=sel).astype(BF16) for xi, m in zip(x, ns)]
        between()
        x = [xi - _pair_dot(xci, xi) for xi, xci in zip(x, xc)]
        between()
        shift += 1
    return x


def _rwkv_kernel(z_ref, mu_ref, kk_ref, ka_ref, rk_ref, w0_ref, wdu_ref, a0_ref, wau_ref,
                 wgu_ref, lnw_ref, lnb_ref, wproj_ref, out_ref, carry_ref, state_ref, y_scr,
                 *staging, width, nblk):
    step = pl.program_id(0)
    nstage = len(staging) // 2

    @pl.when(step == 0)
    def _():
        for ref in (carry_ref, state_ref) + tuple(staging):
            ref[...] = jnp.zeros_like(ref)

    for parity in range(2):
        @pl.when((step & 1) == parity)
        def _(parity=parity):
            _rwkv_step(z_ref, mu_ref, kk_ref, ka_ref, rk_ref, w0_ref, wdu_ref, a0_ref, wau_ref, wgu_ref,
                       lnw_ref, lnb_ref, wproj_ref, out_ref, carry_ref, state_ref, y_scr,
                       staging[parity * nstage:(parity + 1) * nstage],
                       staging[(1 - parity) * nstage:(2 - parity) * nstage], step, width, nblk)


def _rwkv_step(z_ref, mu_ref, kk_ref, ka_ref, rk_ref, w0_ref, wdu_ref, a0_ref, wau_ref, wgu_ref,
               lnw_ref, lnb_ref, wproj_ref, out_ref, carry_ref, state_ref, y_scr, cur, prev, step, width, nblk):
    lhs1_w, rhs1_w, lhs2_w, rt_w, v_w, glast_w, g_w, bonus_w = cur
    lhs1_s, rhs1_s, lhs2_s, rt_s, v_s, glast_s, g_s, bonus_s = prev
    rows = z_ref.shape[0]
    n = RWKV_HEAD
    L = CHUNK
    nchunks = rows // L
    W = width
    PW = 2 * n
    npairs = W // PW
    psl = lambda p: slice(p * PW, (p + 1) * PW)
    csl = lambda c: slice(c * L, (c + 1) * L)

    t = {}

    seq_start = lax.rem(step, nblk) == 0

    def shift_mix(lo, hi):
        z = z_ref[:, lo:hi]
        row_id = _iota2((rows, 1), 0)
        carry = jnp.where(seq_start, 0.0, carry_ref[0:1, lo:hi])
        zprev = jnp.where(row_id == 0, carry, pltpu.roll(z, 1, axis=0))
        carry_ref[0:1, lo:hi] = z[rows - 1:rows, :]
        return z + (zprev - z) * mu_ref[:, lo:hi]

    def task_r():
        t['r'] = shift_mix(0, W)

    def task_k():
        t['k'] = shift_mix(W, 2 * W)

    def task_v():
        t['v'] = shift_mix(2 * W, 3 * W)
        v_w[...] = t['v']

    def task_lora():
        t['zs'] = shift_mix(3 * W, z_ref.shape[1])
        gd = t['zs'][:, DECAY_LORA + AAA_LORA:]
        g_w[...] = _dotb(_sigmoid(gd), wgu_ref[...])

    def task_lw():
        wd = t['zs'][:, :DECAY_LORA]
        t['lw'] = -DECAY_SCALE * _sigmoid(w0_ref[...] + _dotb(jnp.tanh(wd), wdu_ref[...]))

    def task_a():
        ad = t['zs'][:, DECAY_LORA:DECAY_LORA + AAA_LORA]
        t['a'] = _sigmoid(a0_ref[...] + _dotb(ad, wau_ref[...]))

    def task_kk():
        t['kk'] = t['k'] * kk_ref[...]
        t['k_mod'] = t['k'] * (1.0 + (t['a'] - 1.0) * ka_ref[...])

    def task_norm():
        kk_sq = _segment_sums([t['kk'] * t['kk']], n)[0]
        t['kk'] = t['kk'] / jnp.maximum(jnp.sqrt(kk_sq), 1e-12)
        t['b'] = t['kk'] * t['a']

    def task_bonus():
        rk_sum = _segment_sums([t['r'] * t['k_mod'] * rk_ref[...]], n)[0]
        bonus_w[...] = rk_sum * t['v']

    tri_bf = jnp.where(_iota2((L, L), 0) >= _iota2((L, L), 1), 1.0, 0.0).astype(BF16)

    def task_decay(c):
        def run():
            sl = csl(c)
            lw_c = t['lw'][sl]
            cum = _dot_exact(tri_bf, lw_c, 3, m_left=True)
            cum_last = cum[L - 1:L, :]
            t['g_inv', c] = jnp.exp(-cum)
            t['g_end', c] = jnp.exp(cum_last - cum)
            glast_w[c] = jnp.exp(cum_last)
            rt = t['r'][sl] * jnp.exp(cum)
            at = t['kk'][sl] * jnp.exp(cum - lw_c)
            rt_w[sl] = rt
            lhs1_w[c] = jnp.concatenate([at, rt], axis=0).astype(BF16)
        return run

    def dup_t(t1, t2):
        return jnp.concatenate([t1, t1, t2, t2], axis=0).T

    copy_is_head = (_iota2((W, 4 * L), 0) >> 6 & 1) == (_iota2((W, 4 * L), 1) >> 6 & 1)

    def task_rhs1(c):
        def run():
            sl = csl(c)
            bk = dup_t(t['b'][sl] * t['g_inv', c], t['k_mod'][sl] * t['g_inv', c]).astype(BF16)
            rhs1_w[c] = jnp.where(copy_is_head, bk, jnp.zeros_like(bk))
        return run

    first_copy = (_iota2((n, 2 * PW), 1) >> 6 & 1) == 0

    def task_lhs2(c):
        def run():
            sl = csl(c)
            bk = dup_t(t['b'][sl] * t['g_end', c], t['k_mod'][sl] * t['g_end', c])
            for p in range(npairs):
                lhs2_w[c, :, p * 2 * PW:(p + 1) * 2 * PW] = jnp.where(
                    first_copy, bk[p * PW:p * PW + n], bk[p * PW + n:(p + 1) * PW]).astype(BF16)
        return run

    tasks = [task_r, task_k, task_v, task_lora, task_lw, task_a, task_kk, task_norm, task_bonus]
    for c in range(nchunks):
        tasks += [task_decay(c), task_rhs1(c), task_lhs2(c)]
    tasks = iter(tasks)

    def tick(k=1):
        for _ in range(k):
            task = next(tasks, None)
            if task is not None:
                task()

    ri = _iota2((L, 2 * PW), 0)
    ci = _iota2((L, 2 * PW), 1) & (L - 1)
    strict = (ri > ci)[:, :PW]
    incl = ri >= ci
    eye_cat = (_iota2((n, PW), 0) == (_iota2((n, PW), 1) & (n - 1))).astype(F32)
    zeros_bd = jnp.zeros((PW, PW), BF16)
    units = [(c, p) for c in range(nchunks) for p in range(npairs)]
    x1 = [jnp.dot(lhs1_s[c, :, psl(p)], rhs1_s[c, psl(p), :], preferred_element_type=F32)
          for c, p in units]
    n_ab = [m[:L, :PW] for m in x1]
    n_ak = [jnp.where(strict, m[:L, PW:], 0.0).astype(BF16) for m in x1]
    m_rbk = [jnp.where(incl, m[L:, :], 0.0).astype(BF16) for m in x1]
    tick(3)
    t_inv = _unit_lower_inverse(n_ab, between=tick)
    vbd = [_pair_blockdiag(v_s[csl(c), psl(p)]) for c, p in units]
    nv = [_pair_dot(a, b_bd=vb) for a, vb in zip(n_ak, vbd)]
    tick()
    pq = [jnp.dot(tm.astype(BF16),
                  jnp.concatenate([_pair_blockdiag(lhs1_s[c, :L, psl(p)]), _pair_blockdiag(-q)], axis=1),
                  preferred_element_type=F32) for tm, q, (c, p) in zip(t_inv, nv, units)]
    tick()
    r2 = []
    for m, pqi, vb, (c, p) in zip(m_rbk, pq, vbd, units):
        rhs2 = jnp.concatenate(
            [jnp.concatenate([_pair_blockdiag(pqi[:, :PW]), _pair_blockdiag(pqi[:, PW:])], axis=1),
             jnp.concatenate([zeros_bd, vb], axis=1)], axis=0)
        l2 = jnp.concatenate([m, lhs2_s[c, :, p * 2 * PW:(p + 1) * 2 * PW]], axis=0)
        r2.append(jnp.dot(l2, rhs2, preferred_element_type=F32))
    tick()
    lhs3 = [jnp.concatenate([rt_s[csl(c), psl(p)] - m[:L, :PW],
                             eye_cat * glast_s[c][:, psl(p)] - m[L:, :PW]], axis=0)
            for m, (c, p) in zip(r2, units)]

    first = lax.rem(step + nblk - 1, nblk) == 0
    state = [jnp.where(first, 0.0, state_ref[p]) for p in range(npairs)]
    for c in range(nchunks):
        r3 = [_pair_dot(lhs3[c * npairs + p], state[p]) for p in range(npairs)]
        for p in range(npairs):
            m = r2[c * npairs + p]
            y_scr[csl(c), psl(p)] = r3[p][:L] + m[:L, PW:]
            state[p] = r3[p][L:] + m[L:, PW:]
        tick()
    for p in range(npairs):
        state_ref[p] = state[p]
    tick(len(units))

    y = y_scr[...]
    inv_n = 1.0 / n
    yc = y - _segment_sums([y], n)[0] * inv_n
    var = _segment_sums([yc * yc], n)[0] * inv_n
    yn = yc * lax.rsqrt(var + LNX_EPS) * lnw_ref[...] + lnb_ref[...]
    out_ref[...] = _dotb((yn + bonus_s[...]) * g_s[...], wproj_ref[...])


def _rwkv_mix(z_rwkv, seq, layer, mu, k_k, k_a, r_k, w0, wdu, a0, wau, wgu, lnw, lnb, wproj, rows=256):
    t, ncols = z_rwkv.shape
    W = k_k.shape[-1]
    heads = W // RWKV_HEAD
    d = wproj.shape[-1]
    nblk = seq // rows
    nsteps = t // rows
    nchunks = rows // CHUNK
    full = lambda arr: _layer_spec(arr, layer)
    params = (mu, k_k, k_a, r_k, w0, wdu, a0, wau, wgu, lnw, lnb, wproj)
    return pl.pallas_call(
        functools.partial(_rwkv_kernel, width=W, nblk=nblk),
        grid=(nsteps + 1,),
        in_specs=[pl.BlockSpec((rows, ncols), lambda s: (jnp.minimum(s, nsteps - 1), 0))]
        + [full(p) for p in params],
        out_specs=pl.BlockSpec((rows, d), lambda s: (jnp.maximum(s - 1, 0), 0)),
        out_shape=jax.ShapeDtypeStruct((t, d), F32),
        scratch_shapes=[pltpu.VMEM((8, ncols), F32),
                        pltpu.VMEM((heads // 2, RWKV_HEAD, 2 * RWKV_HEAD), F32),
                        pltpu.VMEM((rows, W), F32)] + 2 * [
                        pltpu.VMEM((nchunks, 2 * CHUNK, W), BF16),
                        pltpu.VMEM((nchunks, W, 4 * CHUNK), BF16),
                        pltpu.VMEM((nchunks, RWKV_HEAD, 2 * W), BF16),
                        pltpu.VMEM((rows, W), F32),
                        pltpu.VMEM((rows, W), F32),
                        pltpu.VMEM((nchunks, 1, W), F32),
                        pltpu.VMEM((rows, W), F32),
                        pltpu.VMEM((rows, W), F32)],
        compiler_params=pltpu.CompilerParams(dimension_semantics=("arbitrary",),
                                             vmem_limit_bytes=VMEM_LIMIT),
        name="rwkv_mix",
    )(z_rwkv, *params)


def _s5_disc_kernel(are_ref, aim_ref, ldt_ref, bre_ref, bim_ref, abr_ref, abi_ref, bbr_ref, bbi_ref):
    dt = jnp.exp(ldt_ref[...])
    are = jnp.minimum(are_ref[...], -1e-4)
    aim = aim_ref[...]
    mag = jnp.exp(dt * are)
    abr = mag * jnp.cos(dt * aim)
    abi = mag * jnp.sin(dt * aim)
    den = are * are + aim * aim
    nr = abr - 1.0
    cre = (nr * are + abi * aim) / den
    cim = (abi * are - nr * aim) / den
    abr_ref[...] = abr
    abi_ref[...] = abi
    br = bre_ref[...]
    bi = bim_ref[...]
    bbr_ref[...] = cre[None] * br - cim[None] * bi
    bbi_ref[...] = cre[None] * bi + cim[None] * br


def _s5_disc(a_re, a_im, log_dt, b_re_c, b_im_c):
    G, P = a_re.shape
    C = b_re_c.shape[0]
    return pl.pallas_call(
        _s5_disc_kernel,
        out_shape=[jax.ShapeDtypeStruct((G, P), F32), jax.ShapeDtypeStruct((G, P), F32),
                   jax.ShapeDtypeStruct((C, G, P), F32), jax.ShapeDtypeStruct((C, G, P), F32)],
        name="s5_disc",
    )(a_re, a_im, log_dt.reshape(G, 1), b_re_c, b_im_c)


def _s5_kernel(u_ref, bmat_ref, cmat_ref, abr_ref, abi_ref, dskip_ref, wglu_ref, bglu_ref,
               out_ref, st_r_ref, st_i_ref, x_scr, bt_scr, tb_scr, *, nslab):
    _, nb, rt, _ = u_ref.shape
    rows = nb * rt
    ns = abr_ref.shape[1]
    sw = ns // nslab
    d = out_ref.shape[2]

    @pl.when(pl.program_id(1) == 0)
    def _():
        st_r_ref[...] = jnp.zeros_like(st_r_ref)
        st_i_ref[...] = jnp.zeros_like(st_i_ref)

    bt_scr[...] = u_ref[...].reshape(nslab, rows, 128)
    def input_drive(s):
        for t in range(rt):
            tb_scr[s, t * nb:(t + 1) * nb, :] = bt_scr[s, pl.ds(t, nb, stride=rt), :]
        return _dotb(tb_scr[s], bmat_ref[s])

    ys = []
    bu_next = input_drive(0)
    for s in range(nslab):
        bu = bu_next
        if s + 1 < nslab:
            bu_next = input_drive(s + 1)
        cs = slice(s * sw, (s + 1) * sw)
        ar = abr_ref[:, cs]
        ai = abi_ref[:, cs]
        xr = st_r_ref[:, cs]
        xi = st_i_ref[:, cs]
        for t in range(rt):
            ts = slice(t * nb, (t + 1) * nb)
            xr, xi = ar * xr - ai * xi + bu[ts, :sw], ar * xi + ai * xr + bu[ts, sw:]
            x_scr[ts, :sw] = xr
            x_scr[ts, sw:] = xi
        st_r_ref[:, cs] = xr
        st_i_ref[:, cs] = xi
        tb_scr[s] = _dotb(x_scr[...], cmat_ref[s])
        y_s = jnp.concatenate([tb_scr[s, pl.ds(b, rt, stride=nb), :] for b in range(nb)], axis=0)
        ys.append(y_s + dskip_ref[:, s * 128:(s + 1) * 128] * bt_scr[s])
    y = jnp.concatenate(ys, axis=-1)
    y = 0.5 * y * (1.0 + lax.erf(y * (1.0 / math.sqrt(2.0))))
    zz = _dotb(y, wglu_ref[...]) + bglu_ref[...]
    out_ref[...] = (zz[:, :d] * _sigmoid(zz[:, d:])).reshape(nb, rt, d)


def _s5_mix(u4, layer, bmat, cmat, abr, abi, dskip, wglu, bglu, nb=8, rt=128):
    nslab, bsz, seq, _ = u4.shape
    assert rt % 8 == 0 and bmat.shape[1] == nslab
    ns = abr.shape[-1]
    d = wglu.shape[-1] // 2
    full = lambda arr: _layer_spec(arr, layer)
    params = (bmat, cmat, abr, abi, dskip, wglu, bglu)
    return pl.pallas_call(
        functools.partial(_s5_kernel, nslab=nslab),
        grid=(bsz // nb, seq // rt),
        in_specs=[pl.BlockSpec((nslab, nb, rt, 128), lambda b, j: (0, b, j, 0))] + [full(p) for p in params],
        out_specs=pl.BlockSpec((nb, rt, d), lambda b, j: (b, j, 0)),
        out_shape=jax.ShapeDtypeStruct((bsz, seq, d), F32),
        scratch_shapes=[pltpu.VMEM((nb, ns), F32), pltpu.VMEM((nb, ns), F32),
                        pltpu.VMEM((nb * rt, 2 * ns // nslab), F32),
                        pltpu.VMEM((nslab, nb * rt, 128), F32),
                        pltpu.VMEM((nslab, nb * rt, 128), F32)],
        compiler_params=pltpu.CompilerParams(dimension_semantics=("parallel", "arbitrary"),
                                             vmem_limit_bytes=VMEM_LIMIT),
        name="s5_mix",
    )(u4, *params)


def _merge_ffn_kernel(x_ref, gt_ref, ya_ref, yb_ref, wout_ref, g2_ref, wup_ref, wdn_ref, gf_ref,
                      out_ref, *, final_norm, ff_chunk):
    d = x_ref.shape[1]
    gt = _sigmoid(gt_ref[...])
    m = gt[:, :d] * ya_ref[...] + gt[:, d:] * yb_ref[...]
    x1 = x_ref[...] + _dotb(m, wout_ref[...])
    xn = x1 * lax.rsqrt(jnp.mean(x1 * x1, axis=-1, keepdims=True) + NORM_EPS) * g2_ref[...]
    xn = xn.astype(BF16)
    acc = x1
    dff = wup_ref.shape[1]
    for c in range(dff // ff_chunk):
        cs = slice(c * ff_chunk, (c + 1) * ff_chunk)
        h = jnp.dot(xn, wup_ref[:, cs], preferred_element_type=F32)
        h = jnp.square(jnp.maximum(h, 0.0))
        acc = acc + _dotb(h, wdn_ref[cs, :])
    if final_norm:
        acc = acc * lax.rsqrt(jnp.mean(acc * acc, axis=-1, keepdims=True) + NORM_EPS) * gf_ref[...]
    out_ref[...] = acc


def _merge_ffn(x2, gates, ya, yb, layer, wout, g2, wup, wdn, gf, final_norm, tm=512, ff_chunk=1024):
    t, d = x2.shape
    row = lambda w: pl.BlockSpec((tm, w), lambda i: (i, 0))
    full = lambda arr: _layer_spec(arr, layer)
    return pl.pallas_call(
        functools.partial(_merge_ffn_kernel, final_norm=final_norm, ff_chunk=ff_chunk),
        grid=(t // tm,),
        in_specs=[row(d), row(2 * d), row(d), row(d), full(wout), full(g2), full(wup), full(wdn),
                  pl.BlockSpec(gf.shape, lambda i: (0, 0))],
        out_specs=row(d),
        out_shape=jax.ShapeDtypeStruct((t, d), F32),
        compiler_params=pltpu.CompilerParams(dimension_semantics=("parallel",),
                                             vmem_limit_bytes=VMEM_LIMIT),
        name="merge_ffn",
    )(x2, gates, ya, yb, wout, g2, wup, wdn, gf)


def _s5_layer_params(a_re, a_im, log_dt, b_re, b_im, c_re, c_im):
    G, P, C = b_re.shape
    gps = 128 // C
    nslab = G // gps
    abr, abi, bbr, bbi = _s5_disc(a_re, a_im, log_dt, jnp.transpose(b_re, (2, 0, 1)),
                                  jnp.transpose(b_im, (2, 0, 1)))
    eye = jnp.eye(gps, dtype=F32)

    def bd_in(bb):
        bb = jnp.transpose(bb, (1, 0, 2)).reshape(nslab, gps, C, P)
        return jnp.einsum('sgcp,gh->sgchp', bb, eye).reshape(nslab, gps * C, gps * P)

    def bd_out(cc):
        cc = cc.reshape(nslab, gps, C, P)
        return jnp.einsum('sgcp,gh->sgphc', cc, eye).reshape(nslab, gps * P, gps * C)

    bmat = jnp.concatenate([bd_in(bbr), bd_in(bbi)], axis=2).astype(BF16)
    cmat = jnp.concatenate([bd_out(c_re), -bd_out(c_im)], axis=1).astype(BF16)
    return bmat, cmat, abr.reshape(1, G * P), abi.reshape(1, G * P)


def kernel(x, norm1_g, w_in, mu_shift, k_k, k_a, r_k, w0, w_decay_up, a0, w_aaa_up, w_gate_up,
           lnx_w, lnx_b, w_rwkv_proj, a_re, a_im, log_dt, b_re, b_im, c_re, c_im, d_skip, w_glu,
           b_glu, w_out, norm2_g, w_ff_up, w_ff_down, norm_f_g):
    bsz, seq, d = x.shape
    depth = w_in.shape[0]
    n_shift = mu_shift.shape[1]
    n_ssm = d_skip.shape[1]
    t = bsz * seq
    vec = lambda a: a.reshape(a.shape[0], 1, -1)
    bf = lambda a: a.astype(BF16)
    rwkv_params = (vec(mu_shift), vec(k_k), vec(k_a), vec(r_k), vec(w0), bf(w_decay_up), vec(a0),
                   bf(w_aaa_up), bf(w_gate_up), vec(lnx_w), vec(lnx_b), bf(w_rwkv_proj))
    s5_params = jax.vmap(_s5_layer_params)(a_re, a_im, log_dt, b_re, b_im, c_re, c_im)
    s5_params += (vec(d_skip), bf(w_glu), vec(b_glu))
    w_in_bf, w_out_bf, w_up_bf, w_dn_bf = bf(w_in), bf(w_out), bf(w_ff_up), bf(w_ff_down)
    g1, g2 = vec(norm1_g), vec(norm2_g)
    x2 = x.reshape(t, d)
    for l in range(depth):
        z_rwkv, u, gates = _in_proj(x2, g1, w_in_bf, l, n_shift, n_ssm)
        y_a = _rwkv_mix(z_rwkv, seq, l, *rwkv_params)
        y_b = _s5_mix(u.reshape(-1, bsz, seq, 128), l, *s5_params)
        x2 = _merge_ffn(x2, gates, y_a, y_b.reshape(t, d), l, w_out_bf, g2, w_up_bf,
                        w_dn_bf, norm_f_g.reshape(1, -1), final_norm=(l == depth - 1))
    return x2.reshape(bsz, seq, d)
```

```python
import functools
import math

import jax
import jax.numpy as jnp
from jax import lax
from jax.experimental import pallas as pl
from jax.experimental.pallas import tpu as pltpu

F32 = jnp.float32
BF16 = jnp.bfloat16

RWKV_HEAD = 64
DECAY_LORA = 64
AAA_LORA = 64
LNX_EPS = 64e-5
DECAY_SCALE = math.exp(-0.5)
NORM_EPS = 1e-6

CHUNK = 64
HEAD_SHIFT = RWKV_HEAD.bit_length() - 1
LANES = 128
MXU_TILE = 256
VMEM_LIMIT = 56 * 1024 * 1024


def _dotb(a, b):
    return jnp.dot(a.astype(BF16), b.astype(BF16), preferred_element_type=F32)


def _sigmoid(x):
    return 0.5 * jnp.tanh(0.5 * x) + 0.5


def _iota2(shape, axis):
    return lax.broadcasted_iota(jnp.int32, shape, axis)


def _layer_spec(arr, layer):
    return pl.BlockSpec((None,) + arr.shape[1:], lambda *_: (layer,) + (0,) * (arr.ndim - 1),
                        pipeline_mode=pl.Buffered(1))


def _in_proj_kernel(x_ref, g_ref, w_ref, zr_ref, u_ref, gt_ref, *, n_shift, n_ssm):
    x = x_ref[...]
    xn = x * lax.rsqrt(jnp.mean(x * x, axis=-1, keepdims=True) + NORM_EPS) * g_ref[...]
    z = _dotb(xn, w_ref[...])
    zr_ref[...] = z[:, :n_shift]
    for s in range(u_ref.shape[0]):
        u_ref[s] = z[:, n_shift + s * LANES:n_shift + (s + 1) * LANES]
    gt_ref[...] = z[:, n_shift + n_ssm:]


def _in_proj(x2, g, w_bf, layer, n_shift, n_ssm, tm=512):
    t, d = x2.shape
    n_in = w_bf.shape[-1]
    n_gate = n_in - n_shift - n_ssm
    return pl.pallas_call(
        functools.partial(_in_proj_kernel, n_shift=n_shift, n_ssm=n_ssm),
        grid=(t // tm,),
        in_specs=[pl.BlockSpec((tm, d), lambda i: (i, 0)),
                  _layer_spec(g, layer), _layer_spec(w_bf, layer)],
        out_specs=[pl.BlockSpec((tm, n_shift), lambda i: (i, 0)),
                   pl.BlockSpec((n_ssm // LANES, tm, LANES), lambda i: (0, i, 0)),
                   pl.BlockSpec((tm, n_gate), lambda i: (i, 0))],
        out_shape=[jax.ShapeDtypeStruct((t, n_shift), F32),
                   jax.ShapeDtypeStruct((n_ssm // LANES, t, LANES), F32),
                   jax.ShapeDtypeStruct((t, n_gate), F32)],
        compiler_params=pltpu.CompilerParams(dimension_semantics=("parallel",),
                                             vmem_limit_bytes=VMEM_LIMIT),
        name="in_proj",
    )(x2, g, w_bf)


def _dot_exact(m_bf, x, terms, *, m_left):
    acc = None
    rem = x
    for _ in range(terms):
        piece = rem.astype(BF16)
        d = (jnp.dot(m_bf, piece, preferred_element_type=F32) if m_left
             else jnp.dot(piece, m_bf, preferred_element_type=F32))
        acc = d if acc is None else acc + d
        rem = rem - piece.astype(F32)
    return acc


def _segment_sums(ts, n):
    rows, width = ts[0].shape
    tile = MXU_TILE
    ncol = width // tile
    lg = n.bit_length() - 1
    seg = jnp.where((_iota2((tile, tile), 0) >> lg) == (_iota2((tile, tile), 1) >> lg), 1.0, 0.0).astype(BF16)
    pieces = []
    for t in ts:
        hi = t.astype(BF16)
        lo = (t - hi.astype(F32)).astype(BF16)
        pieces += [part[:, j * tile:(j + 1) * tile] for part in (hi, lo) for j in range(ncol)]
    res = jnp.dot(jnp.concatenate(pieces, axis=0), seg, preferred_element_type=F32)
    blk = lambda i: res[i * rows:(i + 1) * rows]
    return [jnp.concatenate([blk((2 * a) * ncol + j) + blk((2 * a + 1) * ncol + j) for j in range(ncol)], axis=1)
            for a in range(len(ts))]


def _pair_blockdiag(m_cat, mask=None):
    m2 = jnp.concatenate([m_cat, m_cat], axis=0)
    keep = (_iota2(m2.shape, 0) >> HEAD_SHIFT) == (_iota2(m2.shape, 1) >> HEAD_SHIFT)
    if mask is not None:
        keep = keep & jnp.concatenate([mask, mask], axis=0)
    return jnp.where(keep, m2, jnp.zeros_like(m2)).astype(BF16)


def _pair_dot(a_cat, b_cat=None, b_bd=None, mask=None):
    b_bd = _pair_blockdiag(b_cat, mask) if b_bd is None else b_bd
    return jnp.dot(a_cat.astype(BF16), b_bd, preferred_element_type=F32)


def _unit_lower_inverse(ns, between=lambda: None):
    L, lanes = ns[0].shape
    ri = _iota2((L, lanes), 0)
    ci = _iota2((L, lanes), 1) & (L - 1)
    eye = (ri == ci).astype(F32)
    same8 = ((ri >> 3) == (ci >> 3)) & (ri > ci)
    n0 = [jnp.where(same8, m, 0.0) for m in ns]
    x = [eye - m for m in n0]
    p = [_pair_dot(m0, m, mask=same8) for m0, m in zip(n0, ns)]
    between()
    pbd = [_pair_blockdiag(pi) for pi in p]
    x = [xi + _pair_dot(xi, b_bd=pi) for xi, pi in zip(x, pbd)]
    between()
    p = [_pair_dot(pi, b_bd=qi) for pi, qi in zip(p, pbd)]
    between()
    x = [xi + _pair_dot(xi, pi) for xi, pi in zip(x, p)]
    between()
    shift = 4
    while (1 << shift) <= L:
        sel = ((ri >> shift) == (ci >> shift)) & ((ri >> (shift - 1)) > (ci >> (shift - 1)))
        xc = [_pair_dot(xi, m, mask=sel).astype(BF16) for xi, m in zip(x, ns)]
        between()
        x = [xi - _pair_dot(xci, xi) for xi, xci in zip(x, xc)]
        between()
        shift += 1
    return x


def _rwkv_kernel(z_ref, mu_ref, kk_ref, ka_ref, rk_ref, w0_ref, wdu_ref, a0_ref, wau_ref,
                 wgu_ref, lnw_ref, lnb_ref, wproj_ref, out_ref, carry_ref, state_ref, y_scr,
                 *staging, width, nblk):
    step = pl.program_id(0)
    nstage = len(staging) // 2

    @pl.when(step == 0)
    def _():
        for ref in (carry_ref, state_ref) + tuple(staging):
            ref[...] = jnp.zeros_like(ref)

    for parity in range(2):
        @pl.when((step & 1) == parity)
        def _(parity=parity):
            _rwkv_step(z_ref, mu_ref, kk_ref, ka_ref, rk_ref, w0_ref, wdu_ref, a0_ref, wau_ref, wgu_ref,
                       lnw_ref, lnb_ref, wproj_ref, out_ref, carry_ref, state_ref, y_scr,
                       staging[parity * nstage:(parity + 1) * nstage],
                       staging[(1 - parity) * nstage:(2 - parity) * nstage], step, width, nblk)


def _rwkv_step(z_ref, mu_ref, kk_ref, ka_ref, rk_ref, w0_ref, wdu_ref, a0_ref, wau_ref, wgu_ref,
               lnw_ref, lnb_ref, wproj_ref, out_ref, carry_ref, state_ref, y_scr, cur, prev, step, width, nblk):
    lhs1_w, rhs1_w, lhs2_w, rt_w, v_w, glast_w, g_w, bonus_w = cur
    lhs1_s, rhs1_s, lhs2_s, rt_s, v_s, glast_s, g_s, bonus_s = prev
    rows = z_ref.shape[0]
    n = RWKV_HEAD
    L = CHUNK
    nchunks = rows // L
    W = width
    PW = 2 * n
    npairs = W // PW
    psl = lambda p: slice(p * PW, (p + 1) * PW)
    csl = lambda c: slice(c * L, (c + 1) * L)

    t = {}

    seq_start = lax.rem(step, nblk) == 0

    def shift_mix(lo, hi):
        z = z_ref[:, lo:hi]
        row_id = _iota2((rows, 1), 0)
        carry = jnp.where(seq_start, 0.0, carry_ref[0:1, lo:hi])
        zprev = jnp.where(row_id == 0, carry, pltpu.roll(z, 1, axis=0))
        carry_ref[0:1, lo:hi] = z[rows - 1:rows, :]
        return z + (zprev - z) * mu_ref[:, lo:hi]

    def task_r():
        t['r'] = shift_mix(0, W)

    def task_k():
        t['k'] = shift_mix(W, 2 * W)

    def task_v():
        t['v'] = shift_mix(2 * W, 3 * W)
        v_w[...] = t['v']

    def task_lora():
        t['zs'] = shift_mix(3 * W, z_ref.shape[1])
        gd = t['zs'][:, DECAY_LORA + AAA_LORA:]
        g_w[...] = _dotb(_sigmoid(gd), wgu_ref[...])

    def task_lw():
        wd = t['zs'][:, :DECAY_LORA]
        t['lw'] = -DECAY_SCALE * _sigmoid(w0_ref[...] + _dotb(jnp.tanh(wd), wdu_ref[...]))

    def task_a():
        ad = t['zs'][:, DECAY_LORA:DECAY_LORA + AAA_LORA]
        t['a'] = _sigmoid(a0_ref[...] + _dotb(ad, wau_ref[...]))

    def task_kk():
        t['kk'] = t['k'] * kk_ref[...]
        t['k_mod'] = t['k'] * (1.0 + (t['a'] - 1.0) * ka_ref[...])

    def task_norm():
        kk_sq = _segment_sums([t['kk'] * t['kk']], n)[0]
        t['kk'] = t['kk'] / jnp.maximum(jnp.sqrt(kk_sq), 1e-12)
        t['b'] = t['kk'] * t['a']

    def task_bonus():
        rk_sum = _segment_sums([t['r'] * t['k_mod'] * rk_ref[...]], n)[0]
        bonus_w[...] = rk_sum * t['v']

    tri_bf = jnp.where(_iota2((L, L), 0) >= _iota2((L, L), 1), 1.0, 0.0).astype(BF16)

    def task_decay(c):
        def run():
            sl = csl(c)
            lw_c = t['lw'][sl]
            cum = _dot_exact(tri_bf, lw_c, 3, m_left=True)
            cum_last = cum[L - 1:L, :]
            t['g_inv', c] = jnp.exp(-cum)
            t['g_end', c] = jnp.exp(cum_last - cum)
            glast_w[c] = jnp.exp(cum_last)
            rt = t['r'][sl] * jnp.exp(cum)
            at = t['kk'][sl] * jnp.exp(cum - lw_c)
            rt_w[sl] = rt
            lhs1_w[c] = jnp.concatenate([at, rt], axis=0).astype(BF16)
        return run

    def dup_t(t1, t2):
        return jnp.concatenate([t1, t1, t2, t2], axis=0).T

    copy_is_head = (_iota2((W, 4 * L), 0) >> HEAD_SHIFT & 1) == (_iota2((W, 4 * L), 1) >> HEAD_SHIFT & 1)

    def task_rhs1(c):
        def run():
            sl = csl(c)
            bk = dup_t(t['b'][sl] * t['g_inv', c], t['k_mod'][sl] * t['g_inv', c]).astype(BF16)
            rhs1_w[c] = jnp.where(copy_is_head, bk, jnp.zeros_like(bk))
        return run

    first_copy = (_iota2((n, 2 * PW), 1) >> HEAD_SHIFT & 1) == 0

    def task_lhs2(c):
        def run():
            sl = csl(c)
            bk = dup_t(t['b'][sl] * t['g_end', c], t['k_mod'][sl] * t['g_end', c])
            for p in range(npairs):
                lhs2_w[c, :, p * 2 * PW:(p + 1) * 2 * PW] = jnp.where(
                    first_copy, bk[p * PW:p * PW + n], bk[p * PW + n:(p + 1) * PW]).astype(BF16)
        return run

    tasks = [task_r, task_k, task_v, task_lora, task_lw, task_a, task_kk, task_norm, task_bonus]
    for c in range(nchunks):
        tasks += [task_decay(c), task_rhs1(c), task_lhs2(c)]
    tasks = iter(tasks)

    def tick(k=1):
        for _ in range(k):
            task = next(tasks, None)
            if task is not None:
                task()

    ri = _iota2((L, 2 * PW), 0)
    ci = _iota2((L, 2 * PW), 1) & (L - 1)
    strict = (ri > ci)[:, :PW]
    incl = ri >= ci
    eye_cat = (_iota2((n, PW), 0) == (_iota2((n, PW), 1) & (n - 1))).astype(F32)
    zeros_bd = jnp.zeros((PW, PW), BF16)
    units = [(c, p) for c in range(nchunks) for p in range(npairs)]
    x1 = [jnp.dot(lhs1_s[c, :, psl(p)], rhs1_s[c, psl(p), :], preferred_element_type=F32)
          for c, p in units]
    n_ab = [m[:L, :PW] for m in x1]
    n_ak = [jnp.where(strict, m[:L, PW:], 0.0).astype(BF16) for m in x1]
    m_rbk = [jnp.where(incl, m[L:, :], 0.0).astype(BF16) for m in x1]
    tick(3)
    t_inv = _unit_lower_inverse(n_ab, between=tick)
    vbd = [_pair_blockdiag(v_s[csl(c), psl(p)]) for c, p in units]
    nv = [_pair_dot(a, b_bd=vb) for a, vb in zip(n_ak, vbd)]
    tick()
    pq = [jnp.dot(tm.astype(BF16),
                  jnp.concatenate([_pair_blockdiag(lhs1_s[c, :L, psl(p)]), _pair_blockdiag(-q)], axis=1),
                  preferred_element_type=F32) for tm, q, (c, p) in zip(t_inv, nv, units)]
    tick()
    r2 = []
    for m, pqi, vb, (c, p) in zip(m_rbk, pq, vbd, units):
        rhs2 = jnp.concatenate(
            [jnp.concatenate([_pair_blockdiag(pqi[:, :PW]), _pair_blockdiag(pqi[:, PW:])], axis=1),
             jnp.concatenate([zeros_bd, vb], axis=1)], axis=0)
        l2 = jnp.concatenate([m, lhs2_s[c, :, p * 2 * PW:(p + 1) * 2 * PW]], axis=0)
        r2.append(jnp.dot(l2, rhs2, preferred_element_type=F32))
    tick()
    lhs3 = [jnp.concatenate([rt_s[csl(c), psl(p)] - m[:L, :PW],
                             eye_cat * glast_s[c][:, psl(p)] - m[L:, :PW]], axis=0)
            for m, (c, p) in zip(r2, units)]

    first = lax.rem(step + nblk - 1, nblk) == 0
    state = [jnp.where(first, 0.0, state_ref[p]) for p in range(npairs)]
    for c in range(nchunks):
        r3 = [_pair_dot(lhs3[c * npairs + p], state[p]) for p in range(npairs)]
        for p in range(npairs):
            m = r2[c * npairs + p]
            y_scr[csl(c), psl(p)] = r3[p][:L] + m[:L, PW:]
            state[p] = r3[p][L:] + m[L:, PW:]
        tick()
    for p in range(npairs):
        state_ref[p] = state[p]
    tick(len(units))

    y = y_scr[...]
    inv_n = 1.0 / n
    yc = y - _segment_sums([y], n)[0] * inv_n
    var = _segment_sums([yc * yc], n)[0] * inv_n
    yn = yc * lax.rsqrt(var + LNX_EPS) * lnw_ref[...] + lnb_ref[...]
    out_ref[...] = _dotb((yn + bonus_s[...]) * g_s[...], wproj_ref[...])


def _rwkv_mix(z_rwkv, seq, layer, mu, k_k, k_a, r_k, w0, wdu, a0, wau, wgu, lnw, lnb, wproj, rows=256):
    t, ncols = z_rwkv.shape
    W = k_k.shape[-1]
    heads = W // RWKV_HEAD
    d = wproj.shape[-1]
    nblk = seq // rows
    nsteps = t // rows
    nchunks = rows // CHUNK
    full = lambda arr: _layer_spec(arr, layer)
    params = (mu, k_k, k_a, r_k, w0, wdu, a0, wau, wgu, lnw, lnb, wproj)
    return pl.pallas_call(
        functools.partial(_rwkv_kernel, width=W, nblk=nblk),
        grid=(nsteps + 1,),
        in_specs=[pl.BlockSpec((rows, ncols), lambda s: (jnp.minimum(s, nsteps - 1), 0))]
        + [full(p) for p in params],
        out_specs=pl.BlockSpec((rows, d), lambda s: (jnp.maximum(s - 1, 0), 0)),
        out_shape=jax.ShapeDtypeStruct((t, d), F32),
        scratch_shapes=[pltpu.VMEM((8, ncols), F32),
                        pltpu.VMEM((heads // 2, RWKV_HEAD, 2 * RWKV_HEAD), F32),
                        pltpu.VMEM((rows, W), F32)] + 2 * [
                        pltpu.VMEM((nchunks, 2 * CHUNK, W), BF16),
                        pltpu.VMEM((nchunks, W, 4 * CHUNK), BF16),
                        pltpu.VMEM((nchunks, RWKV_HEAD, 2 * W), BF16),
                        pltpu.VMEM((rows, W), F32),
                        pltpu.VMEM((rows, W), F32),
                        pltpu.VMEM((nchunks, 1, W), F32),
                        pltpu.VMEM((rows, W), F32),
                        pltpu.VMEM((rows, W), F32)],
        compiler_params=pltpu.CompilerParams(dimension_semantics=("arbitrary",),
                                             vmem_limit_bytes=VMEM_LIMIT),
        name="rwkv_mix",
    )(z_rwkv, *params)


def _s5_disc_kernel(are_ref, aim_ref, ldt_ref, bre_ref, bim_ref, abr_ref, abi_ref, bbr_ref, bbi_ref):
    dt = jnp.exp(ldt_ref[...])
    are = jnp.minimum(are_ref[...], -1e-4)
    aim = aim_ref[...]
    mag = jnp.exp(dt * are)
    abr = mag * jnp.cos(dt * aim)
    abi = mag * jnp.sin(dt * aim)
    den = are * are + aim * aim
    nr = abr - 1.0
    cre = (nr * are + abi * aim) / den
    cim = (abi * are - nr * aim) / den
    abr_ref[...] = abr
    abi_ref[...] = abi
    br = bre_ref[...]
    bi = bim_ref[...]
    bbr_ref[...] = cre[None] * br - cim[None] * bi
    bbi_ref[...] = cre[None] * bi + cim[None] * br


def _s5_disc(a_re, a_im, log_dt, b_re_c, b_im_c):
    G, P = a_re.shape
    C = b_re_c.shape[0]
    return pl.pallas_call(
        _s5_disc_kernel,
        out_shape=[jax.ShapeDtypeStruct((G, P), F32), jax.ShapeDtypeStruct((G, P), F32),
                   jax.ShapeDtypeStruct((C, G, P), F32), jax.ShapeDtypeStruct((C, G, P), F32)],
        name="s5_disc",
    )(a_re, a_im, log_dt.reshape(G, 1), b_re_c, b_im_c)


def _s5_kernel(u_ref, bmat_ref, cmat_ref, abr_ref, abi_ref, dskip_ref, wglu_ref, bglu_ref,
               out_ref, st_r_ref, st_i_ref, x_scr, bt_scr, tb_scr, *, nslab):
    _, nb, rt, _ = u_ref.shape
    rows = nb * rt
    ns = abr_ref.shape[1]
    sw = ns // nslab
    d = out_ref.shape[2]

    @pl.when(pl.program_id(1) == 0)
    def _():
        st_r_ref[...] = jnp.zeros_like(st_r_ref)
        st_i_ref[...] = jnp.zeros_like(st_i_ref)

    bt_scr[...] = u_ref[...].reshape(nslab, rows, LANES)
    def input_drive(s):
        for t in range(rt):
            tb_scr[s, t * nb:(t + 1) * nb, :] = bt_scr[s, pl.ds(t, nb, stride=rt), :]
        return _dotb(tb_scr[s], bmat_ref[s])

    ys = []
    bu_next = input_drive(0)
    for s in range(nslab):
        bu = bu_next
        if s + 1 < nslab:
            bu_next = input_drive(s + 1)
        cs = slice(s * sw, (s + 1) * sw)
        ar = abr_ref[:, cs]
        ai = abi_ref[:, cs]
        xr = st_r_ref[:, cs]
        xi = st_i_ref[:, cs]
        for t in range(rt):
            ts = slice(t * nb, (t + 1) * nb)
            xr, xi = ar * xr - ai * xi + bu[ts, :sw], ar * xi + ai * xr + bu[ts, sw:]
            x_scr[ts, :sw] = xr
            x_scr[ts, sw:] = xi
        st_r_ref[:, cs] = xr
        st_i_ref[:, cs] = xi
        tb_scr[s] = _dotb(x_scr[...], cmat_ref[s])
        y_s = jnp.concatenate([tb_scr[s, pl.ds(b, rt, stride=nb), :] for b in range(nb)], axis=0)
        ys.append(y_s + dskip_ref[:, s * LANES:(s + 1) * LANES] * bt_scr[s])
    y = jnp.concatenate(ys, axis=-1)
    y = 0.5 * y * (1.0 + lax.erf(y * (1.0 / math.sqrt(2.0))))
    zz = _dotb(y, wglu_ref[...]) + bglu_ref[...]
    out_ref[...] = (zz[:, :d] * _sigmoid(zz[:, d:])).reshape(nb, rt, d)


def _s5_mix(u4, layer, bmat, cmat, abr, abi, dskip, wglu, bglu, nb=8, rt=128):
    nslab, bsz, seq, _ = u4.shape
    assert rt % 8 == 0 and bmat.shape[1] == nslab
    ns = abr.shape[-1]
    d = wglu.shape[-1] // 2
    full = lambda arr: _layer_spec(arr, layer)
    params = (bmat, cmat, abr, abi, dskip, wglu, bglu)
    return pl.pallas_call(
        functools.partial(_s5_kernel, nslab=nslab),
        grid=(bsz // nb, seq // rt),
        in_specs=[pl.BlockSpec((nslab, nb, rt, LANES), lambda b, j: (0, b, j, 0))] + [full(p) for p in params],
        out_specs=pl.BlockSpec((nb, rt, d), lambda b, j: (b, j, 0)),
        out_shape=jax.ShapeDtypeStruct((bsz, seq, d), F32),
        scratch_shapes=[pltpu.VMEM((nb, ns), F32), pltpu.VMEM((nb, ns), F32),
                        pltpu.VMEM((nb * rt, 2 * ns // nslab), F32),
                        pltpu.VMEM((nslab, nb * rt, LANES), F32),
                        pltpu.VMEM((nslab, nb * rt, LANES), F32)],
        compiler_params=pltpu.CompilerParams(dimension_semantics=("parallel", "arbitrary"),
                                             vmem_limit_bytes=VMEM_LIMIT),
        name="s5_mix",
    )(u4, *params)


def _merge_ffn_kernel(x_ref, gt_ref, ya_ref, yb_ref, wout_ref, g2_ref, wup_ref, wdn_ref, gf_ref,
                      out_ref, *, final_norm, ff_chunk):
    d = x_ref.shape[1]
    gt = _sigmoid(gt_ref[...])
    m = gt[:, :d] * ya_ref[...] + gt[:, d:] * yb_ref[...]
    x1 = x_ref[...] + _dotb(m, wout_ref[...])
    xn = x1 * lax.rsqrt(jnp.mean(x1 * x1, axis=-1, keepdims=True) + NORM_EPS) * g2_ref[...]
    xn = xn.astype(BF16)
    acc = x1
    dff = wup_ref.shape[1]
    for c in range(dff // ff_chunk):
        cs = slice(c * ff_chunk, (c + 1) * ff_chunk)
        h = jnp.dot(xn, wup_ref[:, cs], preferred_element_type=F32)
        h = jnp.square(jnp.maximum(h, 0.0))
        acc = acc + _dotb(h, wdn_ref[cs, :])
    if final_norm:
        acc = acc * lax.rsqrt(jnp.mean(acc * acc, axis=-1, keepdims=True) + NORM_EPS) * gf_ref[...]
    out_ref[...] = acc


def _merge_ffn(x2, gates, ya, yb, layer, wout, g2, wup, wdn, gf, final_norm, tm=512, ff_chunk=1024):
    t, d = x2.shape
    row = lambda w: pl.BlockSpec((tm, w), lambda i: (i, 0))
    full = lambda arr: _layer_spec(arr, layer)
    return pl.pallas_call(
        functools.partial(_merge_ffn_kernel, final_norm=final_norm, ff_chunk=ff_chunk),
        grid=(t // tm,),
        in_specs=[row(d), row(2 * d), row(d), row(d), full(wout), full(g2), full(wup), full(wdn),
                  pl.BlockSpec(gf.shape, lambda i: (0, 0))],
        out_specs=row(d),
        out_shape=jax.ShapeDtypeStruct((t, d), F32),
        compiler_params=pltpu.CompilerParams(dimension_semantics=("parallel",),
                                             vmem_limit_bytes=VMEM_LIMIT),
        name="merge_ffn",
    )(x2, gates, ya, yb, wout, g2, wup, wdn, gf)


def _s5_layer_params(a_re, a_im, log_dt, b_re, b_im, c_re, c_im):
    G, P, C = b_re.shape
    gps = LANES // C
    nslab = G // gps
    abr, abi, bbr, bbi = _s5_disc(a_re, a_im, log_dt, jnp.transpose(b_re, (2, 0, 1)),
                                  jnp.transpose(b_im, (2, 0, 1)))
    eye = jnp.eye(gps, dtype=F32)

    def bd_in(bb):
        bb = jnp.transpose(bb, (1, 0, 2)).reshape(nslab, gps, C, P)
        return jnp.einsum('sgcp,gh->sgchp', bb, eye).reshape(nslab, gps * C, gps * P)

    def bd_out(cc):
        cc = cc.reshape(nslab, gps, C, P)
        return jnp.einsum('sgcp,gh->sgphc', cc, eye).reshape(nslab, gps * P, gps * C)

    bmat = jnp.concatenate([bd_in(bbr), bd_in(bbi)], axis=2).astype(BF16)
    cmat = jnp.concatenate([bd_out(c_re), -bd_out(c_im)], axis=1).astype(BF16)
    return bmat, cmat, abr.reshape(1, G * P), abi.reshape(1, G * P)


def kernel(x, norm1_g, w_in, mu_shift, k_k, k_a, r_k, w0, w_decay_up, a0, w_aaa_up, w_gate_up,
           lnx_w, lnx_b, w_rwkv_proj, a_re, a_im, log_dt, b_re, b_im, c_re, c_im, d_skip, w_glu,
           b_glu, w_out, norm2_g, w_ff_up, w_ff_down, norm_f_g):
    bsz, seq, d = x.shape
    depth = w_in.shape[0]
    n_shift = mu_shift.shape[1]
    n_ssm = d_skip.shape[1]
    t = bsz * seq
    vec = lambda a: a.reshape(a.shape[0], 1, -1)
    bf = lambda a: a.astype(BF16)
    rwkv_params = (vec(mu_shift), vec(k_k), vec(k_a), vec(r_k), vec(w0), bf(w_decay_up), vec(a0),
                   bf(w_aaa_up), bf(w_gate_up), vec(lnx_w), vec(lnx_b), bf(w_rwkv_proj))
    s5_params = jax.vmap(_s5_layer_params)(a_re, a_im, log_dt, b_re, b_im, c_re, c_im)
    s5_params += (vec(d_skip), bf(w_glu), vec(b_glu))
    w_in_bf, w_out_bf, w_up_bf, w_dn_bf = bf(w_in), bf(w_out), bf(w_ff_up), bf(w_ff_down)
    g1, g2 = vec(norm1_g), vec(norm2_g)
    x2 = x.reshape(t, d)
    for l in range(depth):
        z_rwkv, u, gates = _in_proj(x2, g1, w_in_bf, l, n_shift, n_ssm)
        y_a = _rwkv_mix(z_rwkv, seq, l, *rwkv_params)
        y_b = _s5_mix(u.reshape(-1, bsz, seq, LANES), l, *s5_params)
        x2 = _merge_ffn(x2, gates, y_a, y_b.reshape(t, d), l, w_out_bf, g2, w_up_bf,
                        w_dn_bf, norm_f_g.reshape(1, -1), final_norm=(l == depth - 1))
    return x2.reshape(bsz, seq, d)
```

```python
import functools
import math

import jax
import jax.numpy as jnp
from jax import lax
from jax.experimental import pallas as pl
from jax.experimental.pallas import tpu as pltpu

F32 = jnp.float32
BF16 = jnp.bfloat16

RWKV_HEAD = 64
DECAY_LORA = 64
AAA_LORA = 64
LNX_EPS = 64e-5
DECAY_SCALE = math.exp(-0.5)
NORM_EPS = 1e-6

CHUNK = 64
HEAD_SHIFT = RWKV_HEAD.bit_length() - 1
LANES = 128
MXU_TILE = 256
VMEM_LIMIT = 56 * 1024 * 1024


def _dotb(a, b):
    return jnp.dot(a.astype(BF16), b.astype(BF16), preferred_element_type=F32)


def _sigmoid(x):
    return 0.5 * jnp.tanh(0.5 * x) + 0.5


def _iota2(shape, axis):
    return lax.broadcasted_iota(jnp.int32, shape, axis)


def _layer_spec(arr, layer):
    return pl.BlockSpec((None,) + arr.shape[1:], lambda *_: (layer,) + (0,) * (arr.ndim - 1),
                        pipeline_mode=pl.Buffered(1))


def _in_proj_kernel(x_ref, g_ref, w_ref, zr_ref, u_ref, gt_ref, *, n_shift, n_ssm):
    x = x_ref[...]
    xn = x * lax.rsqrt(jnp.mean(x * x, axis=-1, keepdims=True) + NORM_EPS) * g_ref[...]
    z = _dotb(xn, w_ref[...])
    zr_ref[...] = z[:, :n_shift]
    for s in range(u_ref.shape[0]):
        u_ref[s] = z[:, n_shift + s * LANES:n_shift + (s + 1) * LANES]
    gt_ref[...] = z[:, n_shift + n_ssm:]


def _in_proj(x2, g, w_bf, layer, n_shift, n_ssm, tm=512):
    t, d = x2.shape
    n_in = w_bf.shape[-1]
    n_gate = n_in - n_shift - n_ssm
    return pl.pallas_call(
        functools.partial(_in_proj_kernel, n_shift=n_shift, n_ssm=n_ssm),
        grid=(t // tm,),
        in_specs=[pl.BlockSpec((tm, d), lambda i: (i, 0)),
                  _layer_spec(g, layer), _layer_spec(w_bf, layer)],
        out_specs=[pl.BlockSpec((tm, n_shift), lambda i: (i, 0)),
                   pl.BlockSpec((n_ssm // LANES, tm, LANES), lambda i: (0, i, 0)),
                   pl.BlockSpec((tm, n_gate), lambda i: (i, 0))],
        out_shape=[jax.ShapeDtypeStruct((t, n_shift), F32),
                   jax.ShapeDtypeStruct((n_ssm // LANES, t, LANES), F32),
                   jax.ShapeDtypeStruct((t, n_gate), F32)],
        compiler_params=pltpu.CompilerParams(dimension_semantics=("parallel",),
                                             vmem_limit_bytes=VMEM_LIMIT),
        name="in_proj",
    )(x2, g, w_bf)


def _dot_exact(m_bf, x, terms, *, m_left):
    acc = None
    rem = x
    for _ in range(terms):
        piece = rem.astype(BF16)
        d = (jnp.dot(m_bf, piece, preferred_element_type=F32) if m_left
             else jnp.dot(piece, m_bf, preferred_element_type=F32))
        acc = d if acc is None else acc + d
        rem = rem - piece.astype(F32)
    return acc


def _segment_sums(ts, n):
    rows, width = ts[0].shape
    tile = MXU_TILE
    ncol = width // tile
    lg = n.bit_length() - 1
    seg = jnp.where((_iota2((tile, tile), 0) >> lg) == (_iota2((tile, tile), 1) >> lg), 1.0, 0.0).astype(BF16)
    pieces = []
    for t in ts:
        hi = t.astype(BF16)
        lo = (t - hi.astype(F32)).astype(BF16)
        pieces += [part[:, j * tile:(j + 1) * tile] for part in (hi, lo) for j in range(ncol)]
    res = jnp.dot(jnp.concatenate(pieces, axis=0), seg, preferred_element_type=F32)
    blk = lambda i: res[i * rows:(i + 1) * rows]
    return [jnp.concatenate([blk((2 * a) * ncol + j) + blk((2 * a + 1) * ncol + j) for j in range(ncol)], axis=1)
            for a in range(len(ts))]


def _pair_blockdiag(m_cat, mask=None):
    m2 = jnp.concatenate([m_cat, m_cat], axis=0)
    keep = (_iota2(m2.shape, 0) >> HEAD_SHIFT) == (_iota2(m2.shape, 1) >> HEAD_SHIFT)
    if mask is not None:
        keep = keep & jnp.concatenate([mask, mask], axis=0)
    return jnp.where(keep, m2, jnp.zeros_like(m2)).astype(BF16)


def _pair_dot(a_cat, b_cat=None, b_bd=None, mask=None):
    b_bd = _pair_blockdiag(b_cat, mask) if b_bd is None else b_bd
    return jnp.dot(a_cat.astype(BF16), b_bd, preferred_element_type=F32)


def _unit_lower_inverse(ns, between=lambda: None):
    L, lanes = ns[0].shape
    ri = _iota2((L, lanes), 0)
    ci = _iota2((L, lanes), 1) & (L - 1)
    eye = (ri == ci).astype(F32)
    same8 = ((ri >> 3) == (ci >> 3)) & (ri > ci)
    n0 = [jnp.where(same8, m, 0.0) for m in ns]
    x = [eye - m for m in n0]
    p = [_pair_dot(m0, m, mask=same8) for m0, m in zip(n0, ns)]
    between()
    pbd = [_pair_blockdiag(pi) for pi in p]
    x = [xi + _pair_dot(xi, b_bd=pi) for xi, pi in zip(x, pbd)]
    between()
    p = [_pair_dot(pi, b_bd=qi) for pi, qi in zip(p, pbd)]
    between()
    x = [xi + _pair_dot(xi, pi) for xi, pi in zip(x, p)]
    between()
    shift = 4
    while (1 << shift) <= L:
        sel = ((ri >> shift) == (ci >> shift)) & ((ri >> (shift - 1)) > (ci >> (shift - 1)))
        xc = [_pair_dot(xi, m, mask=sel).astype(BF16) for xi, m in zip(x, ns)]
        between()
        x = [xi - _pair_dot(xci, xi) for xi, xci in zip(x, xc)]
        between()
        shift += 1
    return x


def _rwkv_kernel(z_ref, mu_ref, kk_ref, ka_ref, rk_ref, w0_ref, wdu_ref, a0_ref, wau_ref,
                 wgu_ref, lnw_ref, lnb_ref, wproj_ref, out_ref, carry_ref, state_ref, y_scr,
                 *staging, width, nblk):
    step = pl.program_id(0)
    nstage = len(staging) // 2

    @pl.when(step == 0)
    def _():
        for ref in (carry_ref, state_ref) + tuple(staging):
            ref[...] = jnp.zeros_like(ref)

    for parity in range(2):
        @pl.when((step & 1) == parity)
        def _(parity=parity):
            _rwkv_step(z_ref, mu_ref, kk_ref, ka_ref, rk_ref, w0_ref, wdu_ref, a0_ref, wau_ref, wgu_ref,
                       lnw_ref, lnb_ref, wproj_ref, out_ref, carry_ref, state_ref, y_scr,
                       staging[parity * nstage:(parity + 1) * nstage],
                       staging[(1 - parity) * nstage:(2 - parity) * nstage], step, width, nblk)


def _rwkv_step(z_ref, mu_ref, kk_ref, ka_ref, rk_ref, w0_ref, wdu_ref, a0_ref, wau_ref, wgu_ref,
               lnw_ref, lnb_ref, wproj_ref, out_ref, carry_ref, state_ref, y_scr, cur, prev, step, width, nblk):
    lhs1_w, rhs1_w, lhs2_w, rt_w, v_w, glast_w, g_w, bonus_w = cur
    lhs1_s, rhs1_s, lhs2_s, rt_s, v_s, glast_s, g_s, bonus_s = prev
    rows = z_ref.shape[0]
    n = RWKV_HEAD
    L = CHUNK
    nchunks = rows // L
    W = width
    PW = 2 * n
    npairs = W // PW
    psl = lambda p: slice(p * PW, (p + 1) * PW)
    csl = lambda c: slice(c * L, (c + 1) * L)

    t = {}

    seq_start = lax.rem(step, nblk) == 0

    def shift_mix(lo, hi):
        z = z_ref[:, lo:hi]
        row_id = _iota2((rows, 1), 0)
        carry = jnp.where(seq_start, 0.0, carry_ref[0:1, lo:hi])
        zprev = jnp.where(row_id == 0, carry, pltpu.roll(z, 1, axis=0))
        carry_ref[0:1, lo:hi] = z[rows - 1:rows, :]
        return z + (zprev - z) * mu_ref[:, lo:hi]

    def task_r():
        t['r'] = shift_mix(0, W)

    def task_k():
        t['k'] = shift_mix(W, 2 * W)

    def task_v():
        t['v'] = shift_mix(2 * W, 3 * W)
        v_w[...] = t['v']

    def task_lora():
        t['zs'] = shift_mix(3 * W, z_ref.shape[1])
        gd = t['zs'][:, DECAY_LORA + AAA_LORA:]
        g_w[...] = _dotb(_sigmoid(gd), wgu_ref[...])

    def task_lw():
        wd = t['zs'][:, :DECAY_LORA]
        t['lw'] = -DECAY_SCALE * _sigmoid(w0_ref[...] + _dotb(jnp.tanh(wd), wdu_ref[...]))

    def task_a():
        ad = t['zs'][:, DECAY_LORA:DECAY_LORA + AAA_LORA]
        t['a'] = _sigmoid(a0_ref[...] + _dotb(ad, wau_ref[...]))

    def task_kk():
        t['kk'] = t['k'] * kk_ref[...]
        t['k_mod'] = t['k'] * (1.0 + (t['a'] - 1.0) * ka_ref[...])

    def task_norm():
        kk_sq = _segment_sums([t['kk'] * t['kk']], n)[0]
        t['kk'] = t['kk'] / jnp.maximum(jnp.sqrt(kk_sq), 1e-12)
        t['b'] = t['kk'] * t['a']

    def task_bonus():
        rk_sum = _segment_sums([t['r'] * t['k_mod'] * rk_ref[...]], n)[0]
        bonus_w[...] = rk_sum * t['v']

    tri_bf = jnp.where(_iota2((L, L), 0) >= _iota2((L, L), 1), 1.0, 0.0).astype(BF16)

    def task_decay(c):
        def run():
            sl = csl(c)
            lw_c = t['lw'][sl]
            cum = _dot_exact(tri_bf, lw_c, 3, m_left=True)
            cum_last = cum[L - 1:L, :]
            t['g_inv', c] = jnp.exp(-cum)
            t['g_end', c] = jnp.exp(cum_last - cum)
            glast_w[c] = jnp.exp(cum_last)
            rt = t['r'][sl] * jnp.exp(cum)
            at = t['kk'][sl] * jnp.exp(cum - lw_c)
            rt_w[sl] = rt
            lhs1_w[c] = jnp.concatenate([at, rt], axis=0).astype(BF16)
        return run

    def dup_t(t1, t2):
        return jnp.concatenate([t1, t1, t2, t2], axis=0).T

    copy_is_head = (_iota2((W, 4 * L), 0) >> HEAD_SHIFT & 1) == (_iota2((W, 4 * L), 1) >> HEAD_SHIFT & 1)

    def task_rhs1(c):
        def run():
            sl = csl(c)
            bk = dup_t(t['b'][sl] * t['g_inv', c], t['k_mod'][sl] * t['g_inv', c]).astype(BF16)
            rhs1_w[c] = jnp.where(copy_is_head, bk, jnp.zeros_like(bk))
        return run

    first_copy = (_iota2((n, 2 * PW), 1) >> HEAD_SHIFT & 1) == 0

    def task_lhs2(c):
        def run():
            sl = csl(c)
            bk = dup_t(t['b'][sl] * t['g_end', c], t['k_mod'][sl] * t['g_end', c])
            for p in range(npairs):
                lhs2_w[c, :, p * 2 * PW:(p + 1) * 2 * PW] = jnp.where(
                    first_copy, bk[p * PW:p * PW + n], bk[p * PW + n:(p + 1) * PW]).astype(BF16)
        return run

    tasks = [task_r, task_k, task_v, task_lora, task_lw, task_a, task_kk, task_norm, task_bonus]
    for c in range(nchunks):
        tasks += [task_decay(c), task_rhs1(c), task_lhs2(c)]
    tasks = iter(tasks)

    def tick(k=1):
        for _ in range(k):
            task = next(tasks, None)
            if task is not None:
                task()

    ri = _iota2((L, 2 * PW), 0)
    ci = _iota2((L, 2 * PW), 1) & (L - 1)
    strict = (ri > ci)[:, :PW]
    incl = ri >= ci
    eye_cat = (_iota2((n, PW), 0) == (_iota2((n, PW), 1) & (n - 1))).astype(F32)
    zeros_bd = jnp.zeros((PW, PW), BF16)
    units = [(c, p) for c in range(nchunks) for p in range(npairs)]
    x1 = [jnp.dot(lhs1_s[c, :, psl(p)], rhs1_s[c, psl(p), :], preferred_element_type=F32)
          for c, p in units]
    n_ab = [m[:L, :PW] for m in x1]
    n_ak = [jnp.where(strict, m[:L, PW:], 0.0).astype(BF16) for m in x1]
    m_rbk = [jnp.where(incl, m[L:, :], 0.0).astype(BF16) for m in x1]
    tick(3)
    t_inv = _unit_lower_inverse(n_ab, between=tick)
    vbd = [_pair_blockdiag(v_s[csl(c), psl(p)]) for c, p in units]
    nv = [_pair_dot(a, b_bd=vb) for a, vb in zip(n_ak, vbd)]
    tick()
    pq = [jnp.dot(tm.astype(BF16),
                  jnp.concatenate([_pair_blockdiag(lhs1_s[c, :L, psl(p)]), _pair_blockdiag(-q)], axis=1),
                  preferred_element_type=F32) for tm, q, (c, p) in zip(t_inv, nv, units)]
    tick()
    r2 = []
    for m, pqi, vb, (c, p) in zip(m_rbk, pq, vbd, units):
        rhs2 = jnp.concatenate(
            [jnp.concatenate([_pair_blockdiag(pqi[:, :PW]), _pair_blockdiag(pqi[:, PW:])], axis=1),
             jnp.concatenate([zeros_bd, vb], axis=1)], axis=0)
        l2 = jnp.concatenate([m, lhs2_s[c, :, p * 2 * PW:(p + 1) * 2 * PW]], axis=0)
        r2.append(jnp.dot(l2, rhs2, preferred_element_type=F32))
    tick()
    lhs3 = [jnp.concatenate([rt_s[csl(c), psl(p)] - m[:L, :PW],
                             eye_cat * glast_s[c][:, psl(p)] - m[L:, :PW]], axis=0)
            for m, (c, p) in zip(r2, units)]

    first = lax.rem(step + nblk - 1, nblk) == 0
    state = [jnp.where(first, 0.0, state_ref[p]) for p in range(npairs)]
    for c in range(nchunks):
        r3 = [_pair_dot(lhs3[c * npairs + p], state[p]) for p in range(npairs)]
        for p in range(npairs):
            m = r2[c * npairs + p]
            y_scr[csl(c), psl(p)] = r3[p][:L] + m[:L, PW:]
            state[p] = r3[p][L:] + m[L:, PW:]
        tick()
    for p in range(npairs):
        state_ref[p] = state[p]
    tick(len(units))

    y = y_scr[...]
    inv_n = 1.0 / n
    yc = y - _segment_sums([y], n)[0] * inv_n
    var = _segment_sums([yc * yc], n)[0] * inv_n
    yn = yc * lax.rsqrt(var + LNX_EPS) * lnw_ref[...] + lnb_ref[...]
    out_ref[...] = _dotb((yn + bonus_s[...]) * g_s[...], wproj_ref[...])


def _rwkv_mix(z_rwkv, seq, layer, mu, k_k, k_a, r_k, w0, wdu, a0, wau, wgu, lnw, lnb, wproj, rows=512):
    t, ncols = z_rwkv.shape
    W = k_k.shape[-1]
    heads = W // RWKV_HEAD
    d = wproj.shape[-1]
    nblk = seq // rows
    nsteps = t // rows
    nchunks = rows // CHUNK
    full = lambda arr: _layer_spec(arr, layer)
    params = (mu, k_k, k_a, r_k, w0, wdu, a0, wau, wgu, lnw, lnb, wproj)
    return pl.pallas_call(
        functools.partial(_rwkv_kernel, width=W, nblk=nblk),
        grid=(nsteps + 1,),
        in_specs=[pl.BlockSpec((rows, ncols), lambda s: (jnp.minimum(s, nsteps - 1), 0))]
        + [full(p) for p in params],
        out_specs=pl.BlockSpec((rows, d), lambda s: (jnp.maximum(s - 1, 0), 0)),
        out_shape=jax.ShapeDtypeStruct((t, d), F32),
        scratch_shapes=[pltpu.VMEM((8, ncols), F32),
                        pltpu.VMEM((heads // 2, RWKV_HEAD, 2 * RWKV_HEAD), F32),
                        pltpu.VMEM((rows, W), F32)] + 2 * [
                        pltpu.VMEM((nchunks, 2 * CHUNK, W), BF16),
                        pltpu.VMEM((nchunks, W, 4 * CHUNK), BF16),
                        pltpu.VMEM((nchunks, RWKV_HEAD, 2 * W), BF16),
                        pltpu.VMEM((rows, W), F32),
                        pltpu.VMEM((rows, W), F32),
                        pltpu.VMEM((nchunks, 1, W), F32),
                        pltpu.VMEM((rows, W), F32),
                        pltpu.VMEM((rows, W), F32)],
        compiler_params=pltpu.CompilerParams(dimension_semantics=("arbitrary",),
                                             vmem_limit_bytes=VMEM_LIMIT),
        name="rwkv_mix",
    )(z_rwkv, *params)


def _s5_disc_kernel(are_ref, aim_ref, ldt_ref, bre_ref, bim_ref, abr_ref, abi_ref, bbr_ref, bbi_ref):
    dt = jnp.exp(ldt_ref[...])
    are = jnp.minimum(are_ref[...], -1e-4)
    aim = aim_ref[...]
    mag = jnp.exp(dt * are)
    abr = mag * jnp.cos(dt * aim)
    abi = mag * jnp.sin(dt * aim)
    den = are * are + aim * aim
    nr = abr - 1.0
    cre = (nr * are + abi * aim) / den
    cim = (abi * are - nr * aim) / den
    abr_ref[...] = abr
    abi_ref[...] = abi
    br = bre_ref[...]
    bi = bim_ref[...]
    bbr_ref[...] = cre[None] * br - cim[None] * bi
    bbi_ref[...] = cre[None] * bi + cim[None] * br


def _s5_disc(a_re, a_im, log_dt, b_re_c, b_im_c):
    G, P = a_re.shape
    C = b_re_c.shape[0]
    return pl.pallas_call(
        _s5_disc_kernel,
        out_shape=[jax.ShapeDtypeStruct((G, P), F32), jax.ShapeDtypeStruct((G, P), F32),
                   jax.ShapeDtypeStruct((C, G, P), F32), jax.ShapeDtypeStruct((C, G, P), F32)],
        name="s5_disc",
    )(a_re, a_im, log_dt.reshape(G, 1), b_re_c, b_im_c)


def _s5_kernel(u_ref, bmat_ref, cmat_ref, abr_ref, abi_ref, dskip_ref, wglu_ref, bglu_ref,
               out_ref, st_r_ref, st_i_ref, x_scr, bt_scr, tb_scr, *, nslab):
    _, nb, rt, _ = u_ref.shape
    rows = nb * rt
    ns = abr_ref.shape[1]
    sw = ns // nslab
    d = out_ref.shape[2]

    @pl.when(pl.program_id(1) == 0)
    def _():
        st_r_ref[...] = jnp.zeros_like(st_r_ref)
        st_i_ref[...] = jnp.zeros_like(st_i_ref)

    bt_scr[...] = u_ref[...].reshape(nslab, rows, LANES)
    def input_drive(s):
        for t in range(rt):
            tb_scr[s, t * nb:(t + 1) * nb, :] = bt_scr[s, pl.ds(t, nb, stride=rt), :]
        return _dotb(tb_scr[s], bmat_ref[s])

    ys = []
    bu_next = input_drive(0)
    for s in range(nslab):
        bu = bu_next
        if s + 1 < nslab:
            bu_next = input_drive(s + 1)
        cs = slice(s * sw, (s + 1) * sw)
        ar = abr_ref[:, cs]
        ai = abi_ref[:, cs]
        xr = st_r_ref[:, cs]
        xi = st_i_ref[:, cs]
        for t in range(rt):
            ts = slice(t * nb, (t + 1) * nb)
            xr, xi = ar * xr - ai * xi + bu[ts, :sw], ar * xi + ai * xr + bu[ts, sw:]
            x_scr[ts, :sw] = xr
            x_scr[ts, sw:] = xi
        st_r_ref[:, cs] = xr
        st_i_ref[:, cs] = xi
        tb_scr[s] = _dotb(x_scr[...], cmat_ref[s])
        y_s = jnp.concatenate([tb_scr[s, pl.ds(b, rt, stride=nb), :] for b in range(nb)], axis=0)
        ys.append(y_s + dskip_ref[:, s * LANES:(s + 1) * LANES] * bt_scr[s])
    y = jnp.concatenate(ys, axis=-1)
    y = 0.5 * y * (1.0 + lax.erf(y * (1.0 / math.sqrt(2.0))))
    zz = _dotb(y, wglu_ref[...]) + bglu_ref[...]
    out_ref[...] = (zz[:, :d] * _sigmoid(zz[:, d:])).reshape(nb, rt, d)


def _s5_mix(u4, layer, bmat, cmat, abr, abi, dskip, wglu, bglu, nb=8, rt=128):
    nslab, bsz, seq, _ = u4.shape
    assert rt % 8 == 0 and bmat.shape[1] == nslab
    ns = abr.shape[-1]
    d = wglu.shape[-1] // 2
    full = lambda arr: _layer_spec(arr, layer)
    params = (bmat, cmat, abr, abi, dskip, wglu, bglu)
    return pl.pallas_call(
        functools.partial(_s5_kernel, nslab=nslab),
        grid=(bsz // nb, seq // rt),
        in_specs=[pl.BlockSpec((nslab, nb, rt, LANES), lambda b, j: (0, b, j, 0))] + [full(p) for p in params],
        out_specs=pl.BlockSpec((nb, rt, d), lambda b, j: (b, j, 0)),
        out_shape=jax.ShapeDtypeStruct((bsz, seq, d), F32),
        scratch_shapes=[pltpu.VMEM((nb, ns), F32), pltpu.VMEM((nb, ns), F32),
                        pltpu.VMEM((nb * rt, 2 * ns // nslab), F32),
                        pltpu.VMEM((nslab, nb * rt, LANES), F32),
                        pltpu.VMEM((nslab, nb * rt, LANES), F32)],
        compiler_params=pltpu.CompilerParams(dimension_semantics=("parallel", "arbitrary"),
                                             vmem_limit_bytes=VMEM_LIMIT),
        name="s5_mix",
    )(u4, *params)


def _merge_ffn_kernel(x_ref, gt_ref, ya_ref, yb_ref, wout_ref, g2_ref, wup_ref, wdn_ref, gf_ref,
                      out_ref, *, final_norm, ff_chunk):
    d = x_ref.shape[1]
    gt = _sigmoid(gt_ref[...])
    m = gt[:, :d] * ya_ref[...] + gt[:, d:] * yb_ref[...]
    x1 = x_ref[...] + _dotb(m, wout_ref[...])
    xn = x1 * lax.rsqrt(jnp.mean(x1 * x1, axis=-1, keepdims=True) + NORM_EPS) * g2_ref[...]
    xn = xn.astype(BF16)
    acc = x1
    dff = wup_ref.shape[1]
    for c in range(dff // ff_chunk):
        cs = slice(c * ff_chunk, (c + 1) * ff_chunk)
        h = jnp.dot(xn, wup_ref[:, cs], preferred_element_type=F32)
        h = jnp.square(jnp.maximum(h, 0.0))
        acc = acc + _dotb(h, wdn_ref[cs, :])
    if final_norm:
        acc = acc * lax.rsqrt(jnp.mean(acc * acc, axis=-1, keepdims=True) + NORM_EPS) * gf_ref[...]
    out_ref[...] = acc


def _merge_ffn(x2, gates, ya, yb, layer, wout, g2, wup, wdn, gf, final_norm, tm=512, ff_chunk=1024):
    t, d = x2.shape
    row = lambda w: pl.BlockSpec((tm, w), lambda i: (i, 0))
    full = lambda arr: _layer_spec(arr, layer)
    return pl.pallas_call(
        functools.partial(_merge_ffn_kernel, final_norm=final_norm, ff_chunk=ff_chunk),
        grid=(t // tm,),
        in_specs=[row(d), row(2 * d), row(d), row(d), full(wout), full(g2), full(wup), full(wdn),
                  pl.BlockSpec(gf.shape, lambda i: (0, 0))],
        out_specs=row(d),
        out_shape=jax.ShapeDtypeStruct((t, d), F32),
        compiler_params=pltpu.CompilerParams(dimension_semantics=("parallel",),
                                             vmem_limit_bytes=VMEM_LIMIT),
        name="merge_ffn",
    )(x2, gates, ya, yb, wout, g2, wup, wdn, gf)


def _s5_layer_params(a_re, a_im, log_dt, b_re, b_im, c_re, c_im):
    G, P, C = b_re.shape
    gps = LANES // C
    nslab = G // gps
    abr, abi, bbr, bbi = _s5_disc(a_re, a_im, log_dt, jnp.transpose(b_re, (2, 0, 1)),
                                  jnp.transpose(b_im, (2, 0, 1)))
    eye = jnp.eye(gps, dtype=F32)

    def bd_in(bb):
        bb = jnp.transpose(bb, (1, 0, 2)).reshape(nslab, gps, C, P)
        return jnp.einsum('sgcp,gh->sgchp', bb, eye).reshape(nslab, gps * C, gps * P)

    def bd_out(cc):
        cc = cc.reshape(nslab, gps, C, P)
        return jnp.einsum('sgcp,gh->sgphc', cc, eye).reshape(nslab, gps * P, gps * C)

    bmat = jnp.concatenate([bd_in(bbr), bd_in(bbi)], axis=2).astype(BF16)
    cmat = jnp.concatenate([bd_out(c_re), -bd_out(c_im)], axis=1).astype(BF16)
    return bmat, cmat, abr.reshape(1, G * P), abi.reshape(1, G * P)


def kernel(x, norm1_g, w_in, mu_shift, k_k, k_a, r_k, w0, w_decay_up, a0, w_aaa_up, w_gate_up,
           lnx_w, lnx_b, w_rwkv_proj, a_re, a_im, log_dt, b_re, b_im, c_re, c_im, d_skip, w_glu,
           b_glu, w_out, norm2_g, w_ff_up, w_ff_down, norm_f_g):
    bsz, seq, d = x.shape
    depth = w_in.shape[0]
    n_shift = mu_shift.shape[1]
    n_ssm = d_skip.shape[1]
    t = bsz * seq
    vec = lambda a: a.reshape(a.shape[0], 1, -1)
    bf = lambda a: a.astype(BF16)
    rwkv_params = (vec(mu_shift), vec(k_k), vec(k_a), vec(r_k), vec(w0), bf(w_decay_up), vec(a0),
                   bf(w_aaa_up), bf(w_gate_up), vec(lnx_w), vec(lnx_b), bf(w_rwkv_proj))
    s5_params = jax.vmap(_s5_layer_params)(a_re, a_im, log_dt, b_re, b_im, c_re, c_im)
    s5_params += (vec(d_skip), bf(w_glu), vec(b_glu))
    w_in_bf, w_out_bf, w_up_bf, w_dn_bf = bf(w_in), bf(w_out), bf(w_ff_up), bf(w_ff_down)
    g1, g2 = vec(norm1_g), vec(norm2_g)
    x2 = x.reshape(t, d)
    for l in range(depth):
        z_rwkv, u, gates = _in_proj(x2, g1, w_in_bf, l, n_shift, n_ssm)
        y_a = _rwkv_mix(z_rwkv, seq, l, *rwkv_params)
        y_b = _s5_mix(u.reshape(-1, bsz, seq, LANES), l, *s5_params)
        x2 = _merge_ffn(x2, gates, y_a, y_b.reshape(t, d), l, w_out_bf, g2, w_up_bf,
                        w_dn_bf, norm_f_g.reshape(1, -1), final_norm=(l == depth - 1))
    return x2.reshape(bsz, seq, d)
```

```python
import functools
import math

import jax
import jax.numpy as jnp
from jax import lax
from jax.experimental import pallas as pl
from jax.experimental.pallas import tpu as pltpu

F32 = jnp.float32
BF16 = jnp.bfloat16

RWKV_HEAD = 64
DECAY_LORA = 64
AAA_LORA = 64
LNX_EPS = 64e-5
DECAY_SCALE = math.exp(-0.5)
NORM_EPS = 1e-6

CHUNK = 64
HEAD_SHIFT = RWKV_HEAD.bit_length() - 1
LANES = 128
MXU_TILE = 256
VMEM_LIMIT = 56 * 1024 * 1024


def _dotb(a, b):
    return jnp.dot(a.astype(BF16), b.astype(BF16), preferred_element_type=F32)


def _sigmoid(x):
    return 0.5 * jnp.tanh(0.5 * x) + 0.5


def _iota2(shape, axis):
    return lax.broadcasted_iota(jnp.int32, shape, axis)


def _layer_spec(arr, layer):
    return pl.BlockSpec((None,) + arr.shape[1:], lambda *_: (layer,) + (0,) * (arr.ndim - 1),
                        pipeline_mode=pl.Buffered(1))


def _in_proj_kernel(x_ref, g_ref, w_ref, zr_ref, u_ref, gt_ref, *, n_shift, n_ssm):
    x = x_ref[...]
    xn = x * lax.rsqrt(jnp.mean(x * x, axis=-1, keepdims=True) + NORM_EPS) * g_ref[...]
    z = _dotb(xn, w_ref[...])
    zr_ref[...] = z[:, :n_shift]
    for s in range(u_ref.shape[0]):
        u_ref[s] = z[:, n_shift + s * LANES:n_shift + (s + 1) * LANES]
    gt_ref[...] = z[:, n_shift + n_ssm:]


def _in_proj(x2, g, w_bf, layer, n_shift, n_ssm, tm=512):
    t, d = x2.shape
    n_in = w_bf.shape[-1]
    n_gate = n_in - n_shift - n_ssm
    return pl.pallas_call(
        functools.partial(_in_proj_kernel, n_shift=n_shift, n_ssm=n_ssm),
        grid=(t // tm,),
        in_specs=[pl.BlockSpec((tm, d), lambda i: (i, 0)),
                  _layer_spec(g, layer), _layer_spec(w_bf, layer)],
        out_specs=[pl.BlockSpec((tm, n_shift), lambda i: (i, 0)),
                   pl.BlockSpec((n_ssm // LANES, tm, LANES), lambda i: (0, i, 0)),
                   pl.BlockSpec((tm, n_gate), lambda i: (i, 0))],
        out_shape=[jax.ShapeDtypeStruct((t, n_shift), F32),
                   jax.ShapeDtypeStruct((n_ssm // LANES, t, LANES), F32),
                   jax.ShapeDtypeStruct((t, n_gate), F32)],
        compiler_params=pltpu.CompilerParams(dimension_semantics=("parallel",),
                                             vmem_limit_bytes=VMEM_LIMIT),
        name="in_proj",
    )(x2, g, w_bf)


def _dot_exact(m_bf, x, terms, *, m_left):
    acc = None
    rem = x
    for _ in range(terms):
        piece = rem.astype(BF16)
        d = (jnp.dot(m_bf, piece, preferred_element_type=F32) if m_left
             else jnp.dot(piece, m_bf, preferred_element_type=F32))
        acc = d if acc is None else acc + d
        rem = rem - piece.astype(F32)
    return acc


def _segment_sums(ts, n):
    rows, width = ts[0].shape
    tile = MXU_TILE
    ncol = width // tile
    lg = n.bit_length() - 1
    seg = jnp.where((_iota2((tile, tile), 0) >> lg) == (_iota2((tile, tile), 1) >> lg), 1.0, 0.0).astype(BF16)
    pieces = []
    for t in ts:
        hi = t.astype(BF16)
        pieces += [hi[:, j * tile:(j + 1) * tile] for j in range(ncol)]
    res = jnp.dot(jnp.concatenate(pieces, axis=0), seg, preferred_element_type=F32)
    blk = lambda i: res[i * rows:(i + 1) * rows]
    return [jnp.concatenate([blk(a * ncol + j) for j in range(ncol)], axis=1)
            for a in range(len(ts))]


def _pair_blockdiag(m_cat, mask=None):
    m2 = jnp.concatenate([m_cat, m_cat], axis=0)
    keep = (_iota2(m2.shape, 0) >> HEAD_SHIFT) == (_iota2(m2.shape, 1) >> HEAD_SHIFT)
    if mask is not None:
        keep = keep & jnp.concatenate([mask, mask], axis=0)
    return jnp.where(keep, m2, jnp.zeros_like(m2)).astype(BF16)


def _pair_dot(a_cat, b_cat=None, b_bd=None, mask=None):
    b_bd = _pair_blockdiag(b_cat, mask) if b_bd is None else b_bd
    return jnp.dot(a_cat.astype(BF16), b_bd, preferred_element_type=F32)


def _unit_lower_inverse(ns, between=lambda: None):
    L, lanes = ns[0].shape
    ri = _iota2((L, lanes), 0)
    ci = _iota2((L, lanes), 1) & (L - 1)
    eye = (ri == ci).astype(F32)
    same8 = ((ri >> 3) == (ci >> 3)) & (ri > ci)
    n0 = [jnp.where(same8, m, 0.0) for m in ns]
    x = [eye - m for m in n0]
    p = [_pair_dot(m0, m, mask=same8) for m0, m in zip(n0, ns)]
    between()
    pbd = [_pair_blockdiag(pi) for pi in p]
    x = [xi + _pair_dot(xi, b_bd=pi) for xi, pi in zip(x, pbd)]
    between()
    p = [_pair_dot(pi, b_bd=qi) for pi, qi in zip(p, pbd)]
    between()
    x = [xi + _pair_dot(xi, pi) for xi, pi in zip(x, p)]
    between()
    shift = 4
    while (1 << shift) <= L:
        sel = ((ri >> shift) == (ci >> shift)) & ((ri >> (shift - 1)) > (ci >> (shift - 1)))
        xc = [_pair_dot(xi, m, mask=sel).astype(BF16) for xi, m in zip(x, ns)]
        between()
        x = [xi - _pair_dot(xci, xi) for xi, xci in zip(x, xc)]
        between()
        shift += 1
    return x


def _rwkv_kernel(z_ref, mu_ref, kk_ref, ka_ref, rk_ref, w0_ref, wdu_ref, a0_ref, wau_ref,
                 wgu_ref, lnw_ref, lnb_ref, wproj_ref, out_ref, carry_ref, state_ref, y_scr,
                 *staging, width, nblk):
    step = pl.program_id(0)
    nstage = len(staging) // 2

    @pl.when(step == 0)
    def _():
        for ref in (carry_ref, state_ref) + tuple(staging):
            ref[...] = jnp.zeros_like(ref)

    for parity in range(2):
        @pl.when((step & 1) == parity)
        def _(parity=parity):
            _rwkv_step(z_ref, mu_ref, kk_ref, ka_ref, rk_ref, w0_ref, wdu_ref, a0_ref, wau_ref, wgu_ref,
                       lnw_ref, lnb_ref, wproj_ref, out_ref, carry_ref, state_ref, y_scr,
                       staging[parity * nstage:(parity + 1) * nstage],
                       staging[(1 - parity) * nstage:(2 - parity) * nstage], step, width, nblk)


def _rwkv_step(z_ref, mu_ref, kk_ref, ka_ref, rk_ref, w0_ref, wdu_ref, a0_ref, wau_ref, wgu_ref,
               lnw_ref, lnb_ref, wproj_ref, out_ref, carry_ref, state_ref, y_scr, cur, prev, step, width, nblk):
    lhs1_w, rhs1_w, lhs2_w, rt_w, v_w, glast_w, g_w, bonus_w = cur
    lhs1_s, rhs1_s, lhs2_s, rt_s, v_s, glast_s, g_s, bonus_s = prev
    rows = z_ref.shape[0]
    n = RWKV_HEAD
    L = CHUNK
    nchunks = rows // L
    W = width
    PW = 2 * n
    npairs = W // PW
    psl = lambda p: slice(p * PW, (p + 1) * PW)
    csl = lambda c: slice(c * L, (c + 1) * L)

    t = {}

    seq_start = lax.rem(step, nblk) == 0

    def shift_mix(lo, hi):
        z = z_ref[:, lo:hi]
        row_id = _iota2((rows, 1), 0)
        carry = jnp.where(seq_start, 0.0, carry_ref[0:1, lo:hi])
        zprev = jnp.where(row_id == 0, carry, pltpu.roll(z, 1, axis=0))
        carry_ref[0:1, lo:hi] = z[rows - 1:rows, :]
        return z + (zprev - z) * mu_ref[:, lo:hi]

    def task_r():
        t['r'] = shift_mix(0, W)

    def task_k():
        t['k'] = shift_mix(W, 2 * W)

    def task_v():
        t['v'] = shift_mix(2 * W, 3 * W)
        v_w[...] = t['v']

    def task_lora():
        t['zs'] = shift_mix(3 * W, z_ref.shape[1])
        gd = t['zs'][:, DECAY_LORA + AAA_LORA:]
        g_w[...] = _dotb(_sigmoid(gd), wgu_ref[...])

    def task_lw():
        wd = t['zs'][:, :DECAY_LORA]
        t['lw'] = -DECAY_SCALE * _sigmoid(w0_ref[...] + _dotb(jnp.tanh(wd), wdu_ref[...]))

    def task_a():
        ad = t['zs'][:, DECAY_LORA:DECAY_LORA + AAA_LORA]
        t['a'] = _sigmoid(a0_ref[...] + _dotb(ad, wau_ref[...]))

    def task_kk():
        t['kk'] = t['k'] * kk_ref[...]
        t['k_mod'] = t['k'] * (1.0 + (t['a'] - 1.0) * ka_ref[...])

    def task_norm():
        kk_sq = _segment_sums([t['kk'] * t['kk']], n)[0]
        t['kk'] = t['kk'] / jnp.maximum(jnp.sqrt(kk_sq), 1e-12)
        t['b'] = t['kk'] * t['a']

    def task_bonus():
        rk_sum = _segment_sums([t['r'] * t['k_mod'] * rk_ref[...]], n)[0]
        bonus_w[...] = rk_sum * t['v']

    tri_bf = jnp.where(_iota2((L, L), 0) >= _iota2((L, L), 1), 1.0, 0.0).astype(BF16)

    def task_decay(c):
        def run():
            sl = csl(c)
            lw_c = t['lw'][sl]
            cum = _dot_exact(tri_bf, lw_c, 2, m_left=True)
            cum_last = cum[L - 1:L, :]
            t['g_inv', c] = jnp.exp(-cum)
            t['g_end', c] = jnp.exp(cum_last - cum)
            glast_w[c] = jnp.exp(cum_last)
            rt = t['r'][sl] * jnp.exp(cum)
            at = t['kk'][sl] * jnp.exp(cum - lw_c)
            rt_w[sl] = rt
            lhs1_w[c] = jnp.concatenate([at, rt], axis=0).astype(BF16)
        return run

    def dup_t(t1, t2):
        return jnp.concatenate([t1, t1, t2, t2], axis=0).T

    copy_is_head = (_iota2((W, 4 * L), 0) >> HEAD_SHIFT & 1) == (_iota2((W, 4 * L), 1) >> HEAD_SHIFT & 1)

    def task_rhs1(c):
        def run():
            sl = csl(c)
            bk = dup_t(t['b'][sl] * t['g_inv', c], t['k_mod'][sl] * t['g_inv', c]).astype(BF16)
            rhs1_w[c] = jnp.where(copy_is_head, bk, jnp.zeros_like(bk))
        return run

    first_copy = (_iota2((n, 2 * PW), 1) >> HEAD_SHIFT & 1) == 0

    def task_lhs2(c):
        def run():
            sl = csl(c)
            bk = dup_t(t['b'][sl] * t['g_end', c], t['k_mod'][sl] * t['g_end', c])
            for p in range(npairs):
                lhs2_w[c, :, p * 2 * PW:(p + 1) * 2 * PW] = jnp.where(
                    first_copy, bk[p * PW:p * PW + n], bk[p * PW + n:(p + 1) * PW]).astype(BF16)
        return run

    tasks = [task_r, task_k, task_v, task_lora, task_lw, task_a, task_kk, task_norm, task_bonus]
    for c in range(nchunks):
        tasks += [task_decay(c), task_rhs1(c), task_lhs2(c)]
    tasks = iter(tasks)

    def tick(k=1):
        for _ in range(k):
            task = next(tasks, None)
            if task is not None:
                task()

    ri = _iota2((L, 2 * PW), 0)
    ci = _iota2((L, 2 * PW), 1) & (L - 1)
    strict = (ri > ci)[:, :PW]
    incl = ri >= ci
    eye_cat = (_iota2((n, PW), 0) == (_iota2((n, PW), 1) & (n - 1))).astype(F32)
    zeros_bd = jnp.zeros((PW, PW), BF16)
    units = [(c, p) for c in range(nchunks) for p in range(npairs)]
    x1 = [jnp.dot(lhs1_s[c, :, psl(p)], rhs1_s[c, psl(p), :], preferred_element_type=F32)
          for c, p in units]
    n_ab = [m[:L, :PW] for m in x1]
    n_ak = [jnp.where(strict, m[:L, PW:], 0.0).astype(BF16) for m in x1]
    m_rbk = [jnp.where(incl, m[L:, :], 0.0).astype(BF16) for m in x1]
    tick(3)
    t_inv = _unit_lower_inverse(n_ab, between=tick)
    vbd = [_pair_blockdiag(v_s[csl(c), psl(p)]) for c, p in units]
    nv = [_pair_dot(a, b_bd=vb) for a, vb in zip(n_ak, vbd)]
    tick()
    pq = [jnp.dot(tm.astype(BF16),
                  jnp.concatenate([_pair_blockdiag(lhs1_s[c, :L, psl(p)]), _pair_blockdiag(-q)], axis=1),
                  preferred_element_type=F32) for tm, q, (c, p) in zip(t_inv, nv, units)]
    tick()
    r2 = []
    for m, pqi, vb, (c, p) in zip(m_rbk, pq, vbd, units):
        rhs2 = jnp.concatenate(
            [jnp.concatenate([_pair_blockdiag(pqi[:, :PW]), _pair_blockdiag(pqi[:, PW:])], axis=1),
             jnp.concatenate([zeros_bd, vb], axis=1)], axis=0)
        l2 = jnp.concatenate([m, lhs2_s[c, :, p * 2 * PW:(p + 1) * 2 * PW]], axis=0)
        r2.append(jnp.dot(l2, rhs2, preferred_element_type=F32))
    tick()
    lhs3 = [jnp.concatenate([rt_s[csl(c), psl(p)] - m[:L, :PW],
                             eye_cat * glast_s[c][:, psl(p)] - m[L:, :PW]], axis=0)
            for m, (c, p) in zip(r2, units)]

    first = lax.rem(step + nblk - 1, nblk) == 0
    state = [jnp.where(first, 0.0, state_ref[p]) for p in range(npairs)]
    for c in range(nchunks):
        r3 = [_pair_dot(lhs3[c * npairs + p], state[p]) for p in range(npairs)]
        for p in range(npairs):
            m = r2[c * npairs + p]
            y_scr[csl(c), psl(p)] = r3[p][:L] + m[:L, PW:]
            state[p] = r3[p][L:] + m[L:, PW:]
        tick()
    for p in range(npairs):
        state_ref[p] = state[p]
    tick(len(units))

    y = y_scr[...]
    inv_n = 1.0 / n
    yc = y - _segment_sums([y], n)[0] * inv_n
    var = _segment_sums([yc * yc], n)[0] * inv_n
    yn = yc * lax.rsqrt(var + LNX_EPS) * lnw_ref[...] + lnb_ref[...]
    out_ref[...] = _dotb((yn + bonus_s[...]) * g_s[...], wproj_ref[...])


def _rwkv_mix(z_rwkv, seq, layer, mu, k_k, k_a, r_k, w0, wdu, a0, wau, wgu, lnw, lnb, wproj, rows=512):
    t, ncols = z_rwkv.shape
    W = k_k.shape[-1]
    heads = W // RWKV_HEAD
    d = wproj.shape[-1]
    nblk = seq // rows
    nsteps = t // rows
    nchunks = rows // CHUNK
    full = lambda arr: _layer_spec(arr, layer)
    params = (mu, k_k, k_a, r_k, w0, wdu, a0, wau, wgu, lnw, lnb, wproj)
    return pl.pallas_call(
        functools.partial(_rwkv_kernel, width=W, nblk=nblk),
        grid=(nsteps + 1,),
        in_specs=[pl.BlockSpec((rows, ncols), lambda s: (jnp.minimum(s, nsteps - 1), 0))]
        + [full(p) for p in params],
        out_specs=pl.BlockSpec((rows, d), lambda s: (jnp.maximum(s - 1, 0), 0)),
        out_shape=jax.ShapeDtypeStruct((t, d), F32),
        scratch_shapes=[pltpu.VMEM((8, ncols), F32),
                        pltpu.VMEM((heads // 2, RWKV_HEAD, 2 * RWKV_HEAD), F32),
                        pltpu.VMEM((rows, W), F32)] + 2 * [
                        pltpu.VMEM((nchunks, 2 * CHUNK, W), BF16),
                        pltpu.VMEM((nchunks, W, 4 * CHUNK), BF16),
                        pltpu.VMEM((nchunks, RWKV_HEAD, 2 * W), BF16),
                        pltpu.VMEM((rows, W), F32),
                        pltpu.VMEM((rows, W), F32),
                        pltpu.VMEM((nchunks, 1, W), F32),
                        pltpu.VMEM((rows, W), F32),
                        pltpu.VMEM((rows, W), F32)],
        compiler_params=pltpu.CompilerParams(dimension_semantics=("arbitrary",),
                                             vmem_limit_bytes=VMEM_LIMIT),
        name="rwkv_mix",
    )(z_rwkv, *params)


def _s5_disc_kernel(are_ref, aim_ref, ldt_ref, bre_ref, bim_ref, abr_ref, abi_ref, bbr_ref, bbi_ref):
    dt = jnp.exp(ldt_ref[...])
    are = jnp.minimum(are_ref[...], -1e-4)
    aim = aim_ref[...]
    mag = jnp.exp(dt * are)
    abr = mag * jnp.cos(dt * aim)
    abi = mag * jnp.sin(dt * aim)
    den = are * are + aim * aim
    nr = abr - 1.0
    cre = (nr * are + abi * aim) / den
    cim = (abi * are - nr * aim) / den
    abr_ref[...] = abr
    abi_ref[...] = abi
    br = bre_ref[...]
    bi = bim_ref[...]
    bbr_ref[...] = cre[None] * br - cim[None] * bi
    bbi_ref[...] = cre[None] * bi + cim[None] * br


def _s5_disc(a_re, a_im, log_dt, b_re_c, b_im_c):
    G, P = a_re.shape
    C = b_re_c.shape[0]
    return pl.pallas_call(
        _s5_disc_kernel,
        out_shape=[jax.ShapeDtypeStruct((G, P), F32), jax.ShapeDtypeStruct((G, P), F32),
                   jax.ShapeDtypeStruct((C, G, P), F32), jax.ShapeDtypeStruct((C, G, P), F32)],
        name="s5_disc",
    )(a_re, a_im, log_dt.reshape(G, 1), b_re_c, b_im_c)


def _s5_kernel(u_ref, bmat_ref, cmat_ref, abr_ref, abi_ref, dskip_ref, wglu_ref, bglu_ref,
               out_ref, st_r_ref, st_i_ref, x_scr, bt_scr, tb_scr, *, nslab):
    _, nb, rt, _ = u_ref.shape
    rows = nb * rt
    ns = abr_ref.shape[1]
    sw = ns // nslab
    d = out_ref.shape[2]

    @pl.when(pl.program_id(1) == 0)
    def _():
        st_r_ref[...] = jnp.zeros_like(st_r_ref)
        st_i_ref[...] = jnp.zeros_like(st_i_ref)

    bt_scr[...] = u_ref[...].reshape(nslab, rows, LANES)
    def input_drive(s):
        for t in range(rt):
            tb_scr[s, t * nb:(t + 1) * nb, :] = bt_scr[s, pl.ds(t, nb, stride=rt), :]
        return _dotb(tb_scr[s], bmat_ref[s])

    ys = []
    bu_next = input_drive(0)
    for s in range(nslab):
        bu = bu_next
        if s + 1 < nslab:
            bu_next = input_drive(s + 1)
        cs = slice(s * sw, (s + 1) * sw)
        ar = abr_ref[:, cs]
        ai = abi_ref[:, cs]
        xr = st_r_ref[:, cs]
        xi = st_i_ref[:, cs]
        for t in range(rt):
            ts = slice(t * nb, (t + 1) * nb)
            xr, xi = ar * xr - ai * xi + bu[ts, :sw], ar * xi + ai * xr + bu[ts, sw:]
            x_scr[ts, :sw] = xr
            x_scr[ts, sw:] = xi
        st_r_ref[:, cs] = xr
        st_i_ref[:, cs] = xi
        tb_scr[s] = _dotb(x_scr[...], cmat_ref[s])
        y_s = jnp.concatenate([tb_scr[s, pl.ds(b, rt, stride=nb), :] for b in range(nb)], axis=0)
        ys.append(y_s + dskip_ref[:, s * LANES:(s + 1) * LANES] * bt_scr[s])
    y = jnp.concatenate(ys, axis=-1)
    y = 0.5 * y * (1.0 + lax.erf(y * (1.0 / math.sqrt(2.0))))
    zz = _dotb(y, wglu_ref[...]) + bglu_ref[...]
    out_ref[...] = (zz[:, :d] * _sigmoid(zz[:, d:])).reshape(nb, rt, d)


def _s5_mix(u4, layer, bmat, cmat, abr, abi, dskip, wglu, bglu, nb=8, rt=128):
    nslab, bsz, seq, _ = u4.shape
    assert rt % 8 == 0 and bmat.shape[1] == nslab
    ns = abr.shape[-1]
    d = wglu.shape[-1] // 2
    full = lambda arr: _layer_spec(arr, layer)
    params = (bmat, cmat, abr, abi, dskip, wglu, bglu)
    return pl.pallas_call(
        functools.partial(_s5_kernel, nslab=nslab),
        grid=(bsz // nb, seq // rt),
        in_specs=[pl.BlockSpec((nslab, nb, rt, LANES), lambda b, j: (0, b, j, 0))] + [full(p) for p in params],
        out_specs=pl.BlockSpec((nb, rt, d), lambda b, j: (b, j, 0)),
        out_shape=jax.ShapeDtypeStruct((bsz, seq, d), F32),
        scratch_shapes=[pltpu.VMEM((nb, ns), F32), pltpu.VMEM((nb, ns), F32),
                        pltpu.VMEM((nb * rt, 2 * ns // nslab), F32),
                        pltpu.VMEM((nslab, nb * rt, LANES), F32),
                        pltpu.VMEM((nslab, nb * rt, LANES), F32)],
        compiler_params=pltpu.CompilerParams(dimension_semantics=("parallel", "arbitrary"),
                                             vmem_limit_bytes=VMEM_LIMIT),
        name="s5_mix",
    )(u4, *params)


def _merge_ffn_kernel(x_ref, gt_ref, ya_ref, yb_ref, wout_ref, g2_ref, wup_ref, wdn_ref, gf_ref,
                      out_ref, *, final_norm, ff_chunk):
    d = x_ref.shape[1]
    gt = _sigmoid(gt_ref[...])
    m = gt[:, :d] * ya_ref[...] + gt[:, d:] * yb_ref[...]
    x1 = x_ref[...] + _dotb(m, wout_ref[...])
    xn = x1 * lax.rsqrt(jnp.mean(x1 * x1, axis=-1, keepdims=True) + NORM_EPS) * g2_ref[...]
    xn = xn.astype(BF16)
    acc = x1
    dff = wup_ref.shape[1]
    for c in range(dff // ff_chunk):
        cs = slice(c * ff_chunk, (c + 1) * ff_chunk)
        h = jnp.dot(xn, wup_ref[:, cs], preferred_element_type=F32)
        h = jnp.square(jnp.maximum(h, 0.0))
        acc = acc + _dotb(h, wdn_ref[cs, :])
    if final_norm:
        acc = acc * lax.rsqrt(jnp.mean(acc * acc, axis=-1, keepdims=True) + NORM_EPS) * gf_ref[...]
    out_ref[...] = acc


def _merge_ffn(x2, gates, ya, yb, layer, wout, g2, wup, wdn, gf, final_norm, tm=512, ff_chunk=1024):
    t, d = x2.shape
    row = lambda w: pl.BlockSpec((tm, w), lambda i: (i, 0))
    full = lambda arr: _layer_spec(arr, layer)
    return pl.pallas_call(
        functools.partial(_merge_ffn_kernel, final_norm=final_norm, ff_chunk=ff_chunk),
        grid=(t // tm,),
        in_specs=[row(d), row(2 * d), row(d), row(d), full(wout), full(g2), full(wup), full(wdn),
                  pl.BlockSpec(gf.shape, lambda i: (0, 0))],
        out_specs=row(d),
        out_shape=jax.ShapeDtypeStruct((t, d), F32),
        compiler_params=pltpu.CompilerParams(dimension_semantics=("parallel",),
                                             vmem_limit_bytes=VMEM_LIMIT),
        name="merge_ffn",
    )(x2, gates, ya, yb, wout, g2, wup, wdn, gf)


def _s5_layer_params(a_re, a_im, log_dt, b_re, b_im, c_re, c_im):
    G, P, C = b_re.shape
    gps = LANES // C
    nslab = G // gps
    abr, abi, bbr, bbi = _s5_disc(a_re, a_im, log_dt, jnp.transpose(b_re, (2, 0, 1)),
                                  jnp.transpose(b_im, (2, 0, 1)))
    eye = jnp.eye(gps, dtype=F32)

    def bd_in(bb):
        bb = jnp.transpose(bb, (1, 0, 2)).reshape(nslab, gps, C, P)
        return jnp.einsum('sgcp,gh->sgchp', bb, eye).reshape(nslab, gps * C, gps * P)

    def bd_out(cc):
        cc = cc.reshape(nslab, gps, C, P)
        return jnp.einsum('sgcp,gh->sgphc', cc, eye).reshape(nslab, gps * P, gps * C)

    bmat = jnp.concatenate([bd_in(bbr), bd_in(bbi)], axis=2).astype(BF16)
    cmat = jnp.concatenate([bd_out(c_re), -bd_out(c_im)], axis=1).astype(BF16)
    return bmat, cmat, abr.reshape(1, G * P), abi.reshape(1, G * P)


def kernel(x, norm1_g, w_in, mu_shift, k_k, k_a, r_k, w0, w_decay_up, a0, w_aaa_up, w_gate_up,
           lnx_w, lnx_b, w_rwkv_proj, a_re, a_im, log_dt, b_re, b_im, c_re, c_im, d_skip, w_glu,
           b_glu, w_out, norm2_g, w_ff_up, w_ff_down, norm_f_g):
    bsz, seq, d = x.shape
    depth = w_in.shape[0]
    n_shift = mu_shift.shape[1]
    n_ssm = d_skip.shape[1]
    t = bsz * seq
    vec = lambda a: a.reshape(a.shape[0], 1, -1)
    bf = lambda a: a.astype(BF16)
    rwkv_params = (vec(mu_shift), vec(k_k), vec(k_a), vec(r_k), vec(w0), bf(w_decay_up), vec(a0),
                   bf(w_aaa_up), bf(w_gate_up), vec(lnx_w), vec(lnx_b), bf(w_rwkv_proj))
    s5_params = jax.vmap(_s5_layer_params)(a_re, a_im, log_dt, b_re, b_im, c_re, c_im)
    s5_params += (vec(d_skip), bf(w_glu), vec(b_glu))
    w_in_bf, w_out_bf, w_up_bf, w_dn_bf = bf(w_in), bf(w_out), bf(w_ff_up), bf(w_ff_down)
    g1, g2 = vec(norm1_g), vec(norm2_g)
    x2 = x.reshape(t, d)
    for l in range(depth):
        z_rwkv, u, gates = _in_proj(x2, g1, w_in_bf, l, n_shift, n_ssm)
        y_a = _rwkv_mix(z_rwkv, seq, l, *rwkv_params)
        y_b = _s5_mix(u.reshape(-1, bsz, seq, LANES), l, *s5_params)
        x2 = _merge_ffn(x2, gates, y_a, y_b.reshape(t, d), l, w_out_bf, g2, w_up_bf,
                        w_dn_bf, norm_f_g.reshape(1, -1), final_norm=(l == depth - 1))
    return x2.reshape(bsz, seq, d)
```

```python
import functools
import math

import jax
import jax.numpy as jnp
from jax import lax
from jax.experimental import pallas as pl
from jax.experimental.pallas import tpu as pltpu

F32 = jnp.float32
BF16 = jnp.bfloat16

RWKV_HEAD = 64
DECAY_LORA = 64
AAA_LORA = 64
LNX_EPS = 64e-5
DECAY_SCALE = math.exp(-0.5)
NORM_EPS = 1e-6

CHUNK = 64
HEAD_SHIFT = RWKV_HEAD.bit_length() - 1
LANES = 128
MXU_TILE = 256
VMEM_LIMIT = 56 * 1024 * 1024


def _dotb(a, b):
    return jnp.dot(a.astype(BF16), b.astype(BF16), preferred_element_type=F32)


def _sigmoid(x):
    return 0.5 * jnp.tanh(0.5 * x) + 0.5


def _iota2(shape, axis):
    return lax.broadcasted_iota(jnp.int32, shape, axis)


def _layer_spec(arr, layer):
    return pl.BlockSpec((None,) + arr.shape[1:], lambda *_: (layer,) + (0,) * (arr.ndim - 1),
                        pipeline_mode=pl.Buffered(1))


def _in_proj_kernel(x_ref, g_ref, w_ref, mu_ref, zr_ref, u_ref, gt_ref, carry_ref, *, n_shift, n_ssm, nblk):
    @pl.when(pl.program_id(0) == 0)
    def _():
        carry_ref[...] = jnp.zeros_like(carry_ref)

    x = x_ref[...]
    xn = x * lax.rsqrt(jnp.mean(x * x, axis=-1, keepdims=True) + NORM_EPS) * g_ref[...]
    z = _dotb(xn, w_ref[...])
    zr = z[:, :n_shift]
    rows = zr.shape[0]
    seq_start = lax.rem(pl.program_id(0), nblk) == 0
    carry = jnp.where(seq_start, 0.0, carry_ref[0:1, :])
    zprev = jnp.where(_iota2((rows, 1), 0) == 0, carry, pltpu.roll(zr, 1, axis=0))
    carry_ref[0:1, :] = zr[rows - 1:rows, :]
    zr_ref[...] = zr + (zprev - zr) * mu_ref[...]
    for s in range(u_ref.shape[0]):
        u_ref[s] = z[:, n_shift + s * LANES:n_shift + (s + 1) * LANES]
    gt_ref[...] = _sigmoid(z[:, n_shift + n_ssm:])


def _in_proj(x2, seq, g, w_bf, mu, layer, n_shift, n_ssm, tm=512):
    t, d = x2.shape
    n_in = w_bf.shape[-1]
    n_gate = n_in - n_shift - n_ssm
    return pl.pallas_call(
        functools.partial(_in_proj_kernel, n_shift=n_shift, n_ssm=n_ssm, nblk=seq // tm),
        grid=(t // tm,),
        in_specs=[pl.BlockSpec((tm, d), lambda i: (i, 0)),
                  _layer_spec(g, layer), _layer_spec(w_bf, layer), _layer_spec(mu, layer)],
        out_specs=[pl.BlockSpec((tm, n_shift), lambda i: (i, 0)),
                   pl.BlockSpec((n_ssm // LANES, tm, LANES), lambda i: (0, i, 0)),
                   pl.BlockSpec((tm, n_gate), lambda i: (i, 0))],
        out_shape=[jax.ShapeDtypeStruct((t, n_shift), F32),
                   jax.ShapeDtypeStruct((n_ssm // LANES, t, LANES), F32),
                   jax.ShapeDtypeStruct((t, n_gate), F32)],
        scratch_shapes=[pltpu.VMEM((8, n_shift), F32)],
        compiler_params=pltpu.CompilerParams(dimension_semantics=("arbitrary",),
                                             vmem_limit_bytes=VMEM_LIMIT),
        name="in_proj",
    )(x2, g, w_bf, mu)


def _dot_exact(m_bf, x, terms, *, m_left):
    acc = None
    rem = x
    for _ in range(terms):
        piece = rem.astype(BF16)
        d = (jnp.dot(m_bf, piece, preferred_element_type=F32) if m_left
             else jnp.dot(piece, m_bf, preferred_element_type=F32))
        acc = d if acc is None else acc + d
        rem = rem - piece.astype(F32)
    return acc


def _segment_sums(ts, n):
    rows, width = ts[0].shape
    tile = MXU_TILE
    ncol = width // tile
    lg = n.bit_length() - 1
    seg = jnp.where((_iota2((tile, tile), 0) >> lg) == (_iota2((tile, tile), 1) >> lg), 1.0, 0.0).astype(BF16)
    pieces = []
    for t in ts:
        hi = t.astype(BF16)
        pieces += [hi[:, j * tile:(j + 1) * tile] for j in range(ncol)]
    res = jnp.dot(jnp.concatenate(pieces, axis=0), seg, preferred_element_type=F32)
    blk = lambda i: res[i * rows:(i + 1) * rows]
    return [jnp.concatenate([blk(a * ncol + j) for j in range(ncol)], axis=1)
            for a in range(len(ts))]


def _pair_blockdiag(m_cat, mask=None):
    m2 = jnp.concatenate([m_cat, m_cat], axis=0)
    keep = (_iota2(m2.shape, 0) >> HEAD_SHIFT) == (_iota2(m2.shape, 1) >> HEAD_SHIFT)
    if mask is not None:
        keep = keep & jnp.concatenate([mask, mask], axis=0)
    return jnp.where(keep, m2, jnp.zeros_like(m2)).astype(BF16)


def _pair_dot(a_cat, b_cat=None, b_bd=None, mask=None):
    b_bd = _pair_blockdiag(b_cat, mask) if b_bd is None else b_bd
    return jnp.dot(a_cat.astype(BF16), b_bd, preferred_element_type=F32)


def _unit_lower_inverse(ns, between=lambda: None):
    L, lanes = ns[0].shape
    ri = _iota2((L, lanes), 0)
    ci = _iota2((L, lanes), 1) & (L - 1)
    eye = (ri == ci).astype(F32)
    same8 = ((ri >> 3) == (ci >> 3)) & (ri > ci)
    n0 = [jnp.where(same8, m, 0.0) for m in ns]
    x = [eye - m for m in n0]
    p = [_pair_dot(m0, m, mask=same8) for m0, m in zip(n0, ns)]
    between()
    pbd = [_pair_blockdiag(pi) for pi in p]
    x = [xi + _pair_dot(xi, b_bd=pi) for xi, pi in zip(x, pbd)]
    between()
    p = [_pair_dot(pi, b_bd=qi) for pi, qi in zip(p, pbd)]
    between()
    x = [xi + _pair_dot(xi, pi) for xi, pi in zip(x, p)]
    between()
    shift = 4
    while (1 << shift) <= L:
        sel = ((ri >> shift) == (ci >> shift)) & ((ri >> (shift - 1)) > (ci >> (shift - 1)))
        xc = [_pair_dot(xi, m, mask=sel).astype(BF16) for xi, m in zip(x, ns)]
        between()
        x = [xi - _pair_dot(xci, xi) for xi, xci in zip(x, xc)]
        between()
        shift += 1
    return x


def _rwkv_kernel(z_ref, kk_ref, ka_ref, rk_ref, w0_ref, wdu_ref, a0_ref, wau_ref,
                 wgu_ref, lnw_ref, lnb_ref, wproj_ref, out_ref, state_ref, y_scr,
                 *staging, width, nblk):
    step = pl.program_id(0)
    nstage = len(staging) // 2

    @pl.when(step == 0)
    def _():
        for ref in (state_ref,) + tuple(staging):
            ref[...] = jnp.zeros_like(ref)

    for parity in range(2):
        @pl.when((step & 1) == parity)
        def _(parity=parity):
            _rwkv_step(z_ref, kk_ref, ka_ref, rk_ref, w0_ref, wdu_ref, a0_ref, wau_ref, wgu_ref,
                       lnw_ref, lnb_ref, wproj_ref, out_ref, state_ref, y_scr,
                       staging[parity * nstage:(parity + 1) * nstage],
                       staging[(1 - parity) * nstage:(2 - parity) * nstage], step, width, nblk)


def _rwkv_step(z_ref, kk_ref, ka_ref, rk_ref, w0_ref, wdu_ref, a0_ref, wau_ref, wgu_ref,
               lnw_ref, lnb_ref, wproj_ref, out_ref, state_ref, y_scr, cur, prev, step, width, nblk):
    lhs1_w, rhs1_w, lhs2_w, rt_w, v_w, glast_w, g_w, bonus_w = cur
    lhs1_s, rhs1_s, lhs2_s, rt_s, v_s, glast_s, g_s, bonus_s = prev
    rows = z_ref.shape[0]
    n = RWKV_HEAD
    L = CHUNK
    nchunks = rows // L
    W = width
    PW = 2 * n
    npairs = W // PW
    psl = lambda p: slice(p * PW, (p + 1) * PW)
    csl = lambda c: slice(c * L, (c + 1) * L)

    t = {}

    def task_r():
        t['r'] = z_ref[:, 0:W]

    def task_k():
        t['k'] = z_ref[:, W:2 * W]

    def task_v():
        t['v'] = z_ref[:, 2 * W:3 * W]
        v_w[...] = t['v']

    def task_lora():
        t['zs'] = z_ref[:, 3 * W:]
        gd = t['zs'][:, DECAY_LORA + AAA_LORA:]
        g_w[...] = _dotb(_sigmoid(gd), wgu_ref[...])

    def task_lw():
        wd = t['zs'][:, :DECAY_LORA]
        t['lw'] = -DECAY_SCALE * _sigmoid(w0_ref[...] + _dotb(jnp.tanh(wd), wdu_ref[...]))

    def task_a():
        ad = t['zs'][:, DECAY_LORA:DECAY_LORA + AAA_LORA]
        t['a'] = _sigmoid(a0_ref[...] + _dotb(ad, wau_ref[...]))

    def task_kk():
        t['kk'] = t['k'] * kk_ref[...]
        t['k_mod'] = t['k'] * (1.0 + (t['a'] - 1.0) * ka_ref[...])

    def task_norm():
        kk_sq = _segment_sums([t['kk'] * t['kk']], n)[0]
        t['kk'] = t['kk'] / jnp.maximum(jnp.sqrt(kk_sq), 1e-12)
        t['b'] = t['kk'] * t['a']

    def task_bonus():
        rk_sum = _segment_sums([t['r'] * t['k_mod'] * rk_ref[...]], n)[0]
        bonus_w[...] = rk_sum * t['v']

    tri_bf = jnp.where(_iota2((L, L), 0) >= _iota2((L, L), 1), 1.0, 0.0).astype(BF16)

    def task_decay(c):
        def run():
            sl = csl(c)
            lw_c = t['lw'][sl]
            cum = _dot_exact(tri_bf, lw_c, 2, m_left=True)
            cum_last = cum[L - 1:L, :]
            t['g_inv', c] = jnp.exp(-cum)
            t['g_end', c] = jnp.exp(cum_last - cum)
            glast_w[c] = jnp.exp(cum_last)
            rt = t['r'][sl] * jnp.exp(cum)
            at = t['kk'][sl] * jnp.exp(cum - lw_c)
            rt_w[sl] = rt
            lhs1_w[c] = jnp.concatenate([at, rt], axis=0).astype(BF16)
        return run

    def dup_t(t1, t2):
        return jnp.concatenate([t1, t1, t2, t2], axis=0).T

    copy_is_head = (_iota2((W, 4 * L), 0) >> HEAD_SHIFT & 1) == (_iota2((W, 4 * L), 1) >> HEAD_SHIFT & 1)

    def task_rhs1(c):
        def run():
            sl = csl(c)
            bk = dup_t(t['b'][sl] * t['g_inv', c], t['k_mod'][sl] * t['g_inv', c]).astype(BF16)
            rhs1_w[c] = jnp.where(copy_is_head, bk, jnp.zeros_like(bk))
        return run

    first_copy = (_iota2((n, 2 * PW), 1) >> HEAD_SHIFT & 1) == 0

    def task_lhs2(c):
        def run():
            sl = csl(c)
            bk = dup_t(t['b'][sl] * t['g_end', c], t['k_mod'][sl] * t['g_end', c])
            for p in range(npairs):
                lhs2_w[c, :, p * 2 * PW:(p + 1) * 2 * PW] = jnp.where(
                    first_copy, bk[p * PW:p * PW + n], bk[p * PW + n:(p + 1) * PW]).astype(BF16)
        return run

    tasks = [task_r, task_k, task_v, task_lora, task_lw, task_a, task_kk, task_norm, task_bonus]
    for c in range(nchunks):
        tasks += [task_decay(c), task_rhs1(c), task_lhs2(c)]
    tasks = iter(tasks)

    def tick(k=1):
        for _ in range(k):
            task = next(tasks, None)
            if task is not None:
                task()

    ri = _iota2((L, 2 * PW), 0)
    ci = _iota2((L, 2 * PW), 1) & (L - 1)
    strict = (ri > ci)[:, :PW]
    incl = ri >= ci
    eye_cat = (_iota2((n, PW), 0) == (_iota2((n, PW), 1) & (n - 1))).astype(F32)
    zeros_bd = jnp.zeros((PW, PW), BF16)
    units = [(c, p) for c in range(nchunks) for p in range(npairs)]
    x1 = [jnp.dot(lhs1_s[c, :, psl(p)], rhs1_s[c, psl(p), :], preferred_element_type=F32)
          for c, p in units]
    n_ab = [m[:L, :PW] for m in x1]
    n_ak = [jnp.where(strict, m[:L, PW:], 0.0).astype(BF16) for m in x1]
    m_rbk = [jnp.where(incl, m[L:, :], 0.0).astype(BF16) for m in x1]
    tick(3)
    t_inv = _unit_lower_inverse(n_ab, between=tick)
    vbd = [_pair_blockdiag(v_s[csl(c), psl(p)]) for c, p in units]
    nv = [_pair_dot(a, b_bd=vb) for a, vb in zip(n_ak, vbd)]
    tick()
    pq = [jnp.dot(tm.astype(BF16),
                  jnp.concatenate([_pair_blockdiag(lhs1_s[c, :L, psl(p)]), _pair_blockdiag(-q)], axis=1),
                  preferred_element_type=F32) for tm, q, (c, p) in zip(t_inv, nv, units)]
    tick()
    r2 = []
    for m, pqi, vb, (c, p) in zip(m_rbk, pq, vbd, units):
        rhs2 = jnp.concatenate(
            [jnp.concatenate([_pair_blockdiag(pqi[:, :PW]), _pair_blockdiag(pqi[:, PW:])], axis=1),
             jnp.concatenate([zeros_bd, vb], axis=1)], axis=0)
        l2 = jnp.concatenate([m, lhs2_s[c, :, p * 2 * PW:(p + 1) * 2 * PW]], axis=0)
        r2.append(jnp.dot(l2, rhs2, preferred_element_type=F32))
    tick()
    lhs3 = [jnp.concatenate([rt_s[csl(c), psl(p)] - m[:L, :PW],
                             eye_cat * glast_s[c][:, psl(p)] - m[L:, :PW]], axis=0)
            for m, (c, p) in zip(r2, units)]

    first = lax.rem(step + nblk - 1, nblk) == 0
    state = [jnp.where(first, 0.0, state_ref[p]) for p in range(npairs)]
    for c in range(nchunks):
        r3 = [_pair_dot(lhs3[c * npairs + p], state[p]) for p in range(npairs)]
        for p in range(npairs):
            m = r2[c * npairs + p]
            y_scr[csl(c), psl(p)] = r3[p][:L] + m[:L, PW:]
            state[p] = r3[p][L:] + m[L:, PW:]
        tick()
    for p in range(npairs):
        state_ref[p] = state[p]
    tick(len(units))

    y = y_scr[...]
    inv_n = 1.0 / n
    yc = y - _segment_sums([y], n)[0] * inv_n
    var = _segment_sums([yc * yc], n)[0] * inv_n
    yn = yc * lax.rsqrt(var + LNX_EPS) * lnw_ref[...] + lnb_ref[...]
    out_ref[...] = _dotb((yn + bonus_s[...]) * g_s[...], wproj_ref[...])


def _rwkv_mix(z_rwkv, seq, layer, k_k, k_a, r_k, w0, wdu, a0, wau, wgu, lnw, lnb, wproj, rows=512):
    t, ncols = z_rwkv.shape
    W = k_k.shape[-1]
    heads = W // RWKV_HEAD
    d = wproj.shape[-1]
    nblk = seq // rows
    nsteps = t // rows
    nchunks = rows // CHUNK
    full = lambda arr: _layer_spec(arr, layer)
    params = (k_k, k_a, r_k, w0, wdu, a0, wau, wgu, lnw, lnb, wproj)
    return pl.pallas_call(
        functools.partial(_rwkv_kernel, width=W, nblk=nblk),
        grid=(nsteps + 1,),
        in_specs=[pl.BlockSpec((rows, ncols), lambda s: (jnp.minimum(s, nsteps - 1), 0))]
        + [full(p) for p in params],
        out_specs=pl.BlockSpec((rows, d), lambda s: (jnp.maximum(s - 1, 0), 0)),
        out_shape=jax.ShapeDtypeStruct((t, d), F32),
        scratch_shapes=[pltpu.VMEM((heads // 2, RWKV_HEAD, 2 * RWKV_HEAD), F32),
                        pltpu.VMEM((rows, W), F32)] + 2 * [
                        pltpu.VMEM((nchunks, 2 * CHUNK, W), BF16),
                        pltpu.VMEM((nchunks, W, 4 * CHUNK), BF16),
                        pltpu.VMEM((nchunks, RWKV_HEAD, 2 * W), BF16),
                        pltpu.VMEM((rows, W), F32),
                        pltpu.VMEM((rows, W), F32),
                        pltpu.VMEM((nchunks, 1, W), F32),
                        pltpu.VMEM((rows, W), F32),
                        pltpu.VMEM((rows, W), F32)],
        compiler_params=pltpu.CompilerParams(dimension_semantics=("arbitrary",),
                                             vmem_limit_bytes=VMEM_LIMIT),
        name="rwkv_mix",
    )(z_rwkv, *params)


def _s5_disc_kernel(are_ref, aim_ref, ldt_ref, bre_ref, bim_ref, abr_ref, abi_ref, bbr_ref, bbi_ref):
    dt = jnp.exp(ldt_ref[...])
    are = jnp.minimum(are_ref[...], -1e-4)
    aim = aim_ref[...]
    mag = jnp.exp(dt * are)
    abr = mag * jnp.cos(dt * aim)
    abi = mag * jnp.sin(dt * aim)
    den = are * are + aim * aim
    nr = abr - 1.0
    cre = (nr * are + abi * aim) / den
    cim = (abi * are - nr * aim) / den
    abr_ref[...] = abr
    abi_ref[...] = abi
    br = bre_ref[...]
    bi = bim_ref[...]
    bbr_ref[...] = cre[None] * br - cim[None] * bi
    bbi_ref[...] = cre[None] * bi + cim[None] * br


def _s5_disc(a_re, a_im, log_dt, b_re_c, b_im_c):
    G, P = a_re.shape
    C = b_re_c.shape[0]
    return pl.pallas_call(
        _s5_disc_kernel,
        out_shape=[jax.ShapeDtypeStruct((G, P), F32), jax.ShapeDtypeStruct((G, P), F32),
                   jax.ShapeDtypeStruct((C, G, P), F32), jax.ShapeDtypeStruct((C, G, P), F32)],
        name="s5_disc",
    )(a_re, a_im, log_dt.reshape(G, 1), b_re_c, b_im_c)


def _s5_kernel(u_ref, bmat_ref, cmat_ref, abr_ref, abi_ref, dskip_ref, wglu_ref, bglu_ref,
               out_ref, st_r_ref, st_i_ref, x_scr, bt_scr, tb_scr, *, nslab):
    _, nb, rt, _ = u_ref.shape
    rows = nb * rt
    ns = abr_ref.shape[1]
    sw = ns // nslab
    d = out_ref.shape[2]

    @pl.when(pl.program_id(1) == 0)
    def _():
        st_r_ref[...] = jnp.zeros_like(st_r_ref)
        st_i_ref[...] = jnp.zeros_like(st_i_ref)

    bt_scr[...] = u_ref[...].reshape(nslab, rows, LANES)
    def input_drive(s):
        for t in range(rt):
            tb_scr[s, t * nb:(t + 1) * nb, :] = bt_scr[s, pl.ds(t, nb, stride=rt), :]
        return _dotb(tb_scr[s], bmat_ref[s])

    ys = []
    bu_next = input_drive(0)
    for s in range(nslab):
        bu = bu_next
        if s + 1 < nslab:
            bu_next = input_drive(s + 1)
        cs = slice(s * sw, (s + 1) * sw)
        ar = abr_ref[:, cs]
        ai = abi_ref[:, cs]
        xr = st_r_ref[:, cs]
        xi = st_i_ref[:, cs]
        for t in range(rt):
            ts = slice(t * nb, (t + 1) * nb)
            xr, xi = ar * xr - ai * xi + bu[ts, :sw], ar * xi + ai * xr + bu[ts, sw:]
            x_scr[ts, :sw] = xr
            x_scr[ts, sw:] = xi
        st_r_ref[:, cs] = xr
        st_i_ref[:, cs] = xi
        tb_scr[s] = _dotb(x_scr[...], cmat_ref[s])
        y_s = jnp.concatenate([tb_scr[s, pl.ds(b, rt, stride=nb), :] for b in range(nb)], axis=0)
        ys.append(y_s + dskip_ref[:, s * LANES:(s + 1) * LANES] * bt_scr[s])
    y = jnp.concatenate(ys, axis=-1)
    y = 0.5 * y * (1.0 + lax.erf(y * (1.0 / math.sqrt(2.0))))
    zz = _dotb(y, wglu_ref[...]) + bglu_ref[...]
    out_ref[...] = (zz[:, :d] * _sigmoid(zz[:, d:])).reshape(nb, rt, d)


def _s5_mix(u4, layer, bmat, cmat, abr, abi, dskip, wglu, bglu, nb=8, rt=128):
    nslab, bsz, seq, _ = u4.shape
    assert rt % 8 == 0 and bmat.shape[1] == nslab
    ns = abr.shape[-1]
    d = wglu.shape[-1] // 2
    full = lambda arr: _layer_spec(arr, layer)
    params = (bmat, cmat, abr, abi, dskip, wglu, bglu)
    return pl.pallas_call(
        functools.partial(_s5_kernel, nslab=nslab),
        grid=(bsz // nb, seq // rt),
        in_specs=[pl.BlockSpec((nslab, nb, rt, LANES), lambda b, j: (0, b, j, 0))] + [full(p) for p in params],
        out_specs=pl.BlockSpec((nb, rt, d), lambda b, j: (b, j, 0)),
        out_shape=jax.ShapeDtypeStruct((bsz, seq, d), F32),
        scratch_shapes=[pltpu.VMEM((nb, ns), F32), pltpu.VMEM((nb, ns), F32),
                        pltpu.VMEM((nb * rt, 2 * ns // nslab), F32),
                        pltpu.VMEM((nslab, nb * rt, LANES), F32),
                        pltpu.VMEM((nslab, nb * rt, LANES), F32)],
        compiler_params=pltpu.CompilerParams(dimension_semantics=("parallel", "arbitrary"),
                                             vmem_limit_bytes=VMEM_LIMIT),
        name="s5_mix",
    )(u4, *params)


def _merge_ffn_kernel(x_ref, gt_ref, ya_ref, yb_ref, wout_ref, g2_ref, wup_ref, wdn_ref, gf_ref,
                      out_ref, *, final_norm, ff_chunk):
    d = x_ref.shape[1]
    gt = gt_ref[...]
    m = gt[:, :d] * ya_ref[...] + gt[:, d:] * yb_ref[...]
    x1 = x_ref[...] + _dotb(m, wout_ref[...])
    xn = x1 * lax.rsqrt(jnp.mean(x1 * x1, axis=-1, keepdims=True) + NORM_EPS) * g2_ref[...]
    xn = xn.astype(BF16)
    acc = x1
    dff = wup_ref.shape[1]
    for c in range(dff // ff_chunk):
        cs = slice(c * ff_chunk, (c + 1) * ff_chunk)
        h = jnp.dot(xn, wup_ref[:, cs], preferred_element_type=F32)
        h = jnp.square(jnp.maximum(h, 0.0))
        acc = acc + _dotb(h, wdn_ref[cs, :])
    if final_norm:
        acc = acc * lax.rsqrt(jnp.mean(acc * acc, axis=-1, keepdims=True) + NORM_EPS) * gf_ref[...]
    out_ref[...] = acc


def _merge_ffn(x2, gates, ya, yb, layer, wout, g2, wup, wdn, gf, final_norm, tm=512, ff_chunk=1024):
    t, d = x2.shape
    row = lambda w: pl.BlockSpec((tm, w), lambda i: (i, 0))
    full = lambda arr: _layer_spec(arr, layer)
    return pl.pallas_call(
        functools.partial(_merge_ffn_kernel, final_norm=final_norm, ff_chunk=ff_chunk),
        grid=(t // tm,),
        in_specs=[row(d), row(2 * d), row(d), row(d), full(wout), full(g2), full(wup), full(wdn),
                  pl.BlockSpec(gf.shape, lambda i: (0, 0))],
        out_specs=row(d),
        out_shape=jax.ShapeDtypeStruct((t, d), F32),
        compiler_params=pltpu.CompilerParams(dimension_semantics=("parallel",),
                                             vmem_limit_bytes=VMEM_LIMIT),
        name="merge_ffn",
    )(x2, gates, ya, yb, wout, g2, wup, wdn, gf)


def _s5_layer_params(a_re, a_im, log_dt, b_re, b_im, c_re, c_im):
    G, P, C = b_re.shape
    gps = LANES // C
    nslab = G // gps
    abr, abi, bbr, bbi = _s5_disc(a_re, a_im, log_dt, jnp.transpose(b_re, (2, 0, 1)),
                                  jnp.transpose(b_im, (2, 0, 1)))
    eye = jnp.eye(gps, dtype=F32)

    def bd_in(bb):
        bb = jnp.transpose(bb, (1, 0, 2)).reshape(nslab, gps, C, P)
        return jnp.einsum('sgcp,gh->sgchp', bb, eye).reshape(nslab, gps * C, gps * P)

    def bd_out(cc):
        cc = cc.reshape(nslab, gps, C, P)
        return jnp.einsum('sgcp,gh->sgphc', cc, eye).reshape(nslab, gps * P, gps * C)

    bmat = jnp.concatenate([bd_in(bbr), bd_in(bbi)], axis=2).astype(BF16)
    cmat = jnp.concatenate([bd_out(c_re), -bd_out(c_im)], axis=1).astype(BF16)
    return bmat, cmat, abr.reshape(1, G * P), abi.reshape(1, G * P)


def kernel(x, norm1_g, w_in, mu_shift, k_k, k_a, r_k, w0, w_decay_up, a0, w_aaa_up, w_gate_up,
           lnx_w, lnx_b, w_rwkv_proj, a_re, a_im, log_dt, b_re, b_im, c_re, c_im, d_skip, w_glu,
           b_glu, w_out, norm2_g, w_ff_up, w_ff_down, norm_f_g):
    bsz, seq, d = x.shape
    depth = w_in.shape[0]
    n_shift = mu_shift.shape[1]
    n_ssm = d_skip.shape[1]
    t = bsz * seq
    vec = lambda a: a.reshape(a.shape[0], 1, -1)
    bf = lambda a: a.astype(BF16)
    rwkv_params = (vec(k_k), vec(k_a), vec(r_k), vec(w0), bf(w_decay_up), vec(a0),
                   bf(w_aaa_up), bf(w_gate_up), vec(lnx_w), vec(lnx_b), bf(w_rwkv_proj))
    s5_params = jax.vmap(_s5_layer_params)(a_re, a_im, log_dt, b_re, b_im, c_re, c_im)
    s5_params += (vec(d_skip), bf(w_glu), vec(b_glu))
    w_in_bf, w_out_bf, w_up_bf, w_dn_bf = bf(w_in), bf(w_out), bf(w_ff_up), bf(w_ff_down)
    g1, g2, mu = vec(norm1_g), vec(norm2_g), vec(mu_shift)
    x2 = x.reshape(t, d)
    for l in range(depth):
        z_rwkv, u, gates = _in_proj(x2, seq, g1, w_in_bf, mu, l, n_shift, n_ssm)
        y_a = _rwkv_mix(z_rwkv, seq, l, *rwkv_params)
        y_b = _s5_mix(u.reshape(-1, bsz, seq, LANES), l, *s5_params)
        x2 = _merge_ffn(x2, gates, y_a, y_b.reshape(t, d), l, w_out_bf, g2, w_up_bf,
                        w_dn_bf, norm_f_g.reshape(1, -1), final_norm=(l == depth - 1))
    return x2.reshape(bsz, seq, d)
```

```python
import functools
import math

import jax
import jax.numpy as jnp
from jax import lax
from jax.experimental import pallas as pl
from jax.experimental.pallas import tpu as pltpu

F32 = jnp.float32
BF16 = jnp.bfloat16

RWKV_HEAD = 64
DECAY_LORA = 64
AAA_LORA = 64
LNX_EPS = 64e-5
DECAY_SCALE = math.exp(-0.5)
NORM_EPS = 1e-6

CHUNK = 64
HEAD_SHIFT = RWKV_HEAD.bit_length() - 1
LANES = 128
MXU_TILE = 256
VMEM_LIMIT = 56 * 1024 * 1024


def _dotb(a, b):
    return jnp.dot(a.astype(BF16), b.astype(BF16), preferred_element_type=F32)


def _sigmoid(x):
    return 0.5 * jnp.tanh(0.5 * x) + 0.5


def _iota2(shape, axis):
    return lax.broadcasted_iota(jnp.int32, shape, axis)


def _layer_spec(arr, layer):
    return pl.BlockSpec((None,) + arr.shape[1:], lambda *_: (layer,) + (0,) * (arr.ndim - 1),
                        pipeline_mode=pl.Buffered(1))


def _in_proj_kernel(x_ref, g_ref, w_ref, mu_ref, zr_ref, u_ref, gt_ref, carry_ref, *, n_shift, n_ssm, nblk):
    @pl.when(pl.program_id(0) == 0)
    def _():
        carry_ref[...] = jnp.zeros_like(carry_ref)

    x = x_ref[...]
    xn = x * lax.rsqrt(jnp.mean(x * x, axis=-1, keepdims=True) + NORM_EPS) * g_ref[...]
    z = _dotb(xn, w_ref[...])
    zr = z[:, :n_shift]
    rows = zr.shape[0]
    seq_start = lax.rem(pl.program_id(0), nblk) == 0
    carry = jnp.where(seq_start, 0.0, carry_ref[0:1, :])
    zprev = jnp.where(_iota2((rows, 1), 0) == 0, carry, pltpu.roll(zr, 1, axis=0))
    carry_ref[0:1, :] = zr[rows - 1:rows, :]
    zr_ref[...] = zr + (zprev - zr) * mu_ref[...]
    for s in range(u_ref.shape[0]):
        u_ref[s] = z[:, n_shift + s * LANES:n_shift + (s + 1) * LANES]
    gt_ref[...] = _sigmoid(z[:, n_shift + n_ssm:])


def _in_proj(x2, seq, g, w_bf, mu, layer, n_shift, n_ssm, tm=512):
    t, d = x2.shape
    n_in = w_bf.shape[-1]
    n_gate = n_in - n_shift - n_ssm
    return pl.pallas_call(
        functools.partial(_in_proj_kernel, n_shift=n_shift, n_ssm=n_ssm, nblk=seq // tm),
        grid=(t // tm,),
        in_specs=[pl.BlockSpec((tm, d), lambda i: (i, 0)),
                  _layer_spec(g, layer), _layer_spec(w_bf, layer), _layer_spec(mu, layer)],
        out_specs=[pl.BlockSpec((tm, n_shift), lambda i: (i, 0)),
                   pl.BlockSpec((n_ssm // LANES, tm, LANES), lambda i: (0, i, 0)),
                   pl.BlockSpec((tm, n_gate), lambda i: (i, 0))],
        out_shape=[jax.ShapeDtypeStruct((t, n_shift), F32),
                   jax.ShapeDtypeStruct((n_ssm // LANES, t, LANES), F32),
                   jax.ShapeDtypeStruct((t, n_gate), F32)],
        scratch_shapes=[pltpu.VMEM((8, n_shift), F32)],
        compiler_params=pltpu.CompilerParams(dimension_semantics=("arbitrary",),
                                             vmem_limit_bytes=VMEM_LIMIT),
        name="in_proj",
    )(x2, g, w_bf, mu)


def _dot_exact(m_bf, x, terms, *, m_left):
    acc = None
    rem = x
    for _ in range(terms):
        piece = rem.astype(BF16)
        d = (jnp.dot(m_bf, piece, preferred_element_type=F32) if m_left
             else jnp.dot(piece, m_bf, preferred_element_type=F32))
        acc = d if acc is None else acc + d
        rem = rem - piece.astype(F32)
    return acc


def _segment_sums(ts, n):
    rows, width = ts[0].shape
    tile = MXU_TILE
    ncol = width // tile
    lg = n.bit_length() - 1
    seg = jnp.where((_iota2((tile, tile), 0) >> lg) == (_iota2((tile, tile), 1) >> lg), 1.0, 0.0).astype(BF16)
    pieces = []
    for t in ts:
        hi = t.astype(BF16)
        pieces += [hi[:, j * tile:(j + 1) * tile] for j in range(ncol)]
    res = jnp.dot(jnp.concatenate(pieces, axis=0), seg, preferred_element_type=F32)
    blk = lambda i: res[i * rows:(i + 1) * rows]
    return [jnp.concatenate([blk(a * ncol + j) for j in range(ncol)], axis=1)
            for a in range(len(ts))]


def _pair_blockdiag(m_cat, mask=None):
    m2 = jnp.concatenate([m_cat, m_cat], axis=0)
    keep = (_iota2(m2.shape, 0) >> HEAD_SHIFT) == (_iota2(m2.shape, 1) >> HEAD_SHIFT)
    if mask is not None:
        keep = keep & jnp.concatenate([mask, mask], axis=0)
    return jnp.where(keep, m2, jnp.zeros_like(m2)).astype(BF16)


def _pair_dot(a_cat, b_cat=None, b_bd=None, mask=None):
    b_bd = _pair_blockdiag(b_cat, mask) if b_bd is None else b_bd
    return jnp.dot(a_cat.astype(BF16), b_bd, preferred_element_type=F32)


def _unit_lower_inverse(ns, between=lambda: None):
    L, lanes = ns[0].shape
    ri = _iota2((L, lanes), 0)
    ci = _iota2((L, lanes), 1) & (L - 1)
    eye = (ri == ci).astype(F32)
    same8 = ((ri >> 3) == (ci >> 3)) & (ri > ci)
    n0 = [jnp.where(same8, m, 0.0) for m in ns]
    x = [eye - m for m in n0]
    p = [_pair_dot(m0, m, mask=same8) for m0, m in zip(n0, ns)]
    between()
    pbd = [_pair_blockdiag(pi) for pi in p]
    x = [xi + _pair_dot(xi, b_bd=pi) for xi, pi in zip(x, pbd)]
    between()
    p = [_pair_dot(pi, b_bd=qi) for pi, qi in zip(p, pbd)]
    between()
    x = [xi + _pair_dot(xi, pi) for xi, pi in zip(x, p)]
    between()
    shift = 4
    while (1 << shift) <= L:
        sel = ((ri >> shift) == (ci >> shift)) & ((ri >> (shift - 1)) > (ci >> (shift - 1)))
        xc = [_pair_dot(xi, m, mask=sel).astype(BF16) for xi, m in zip(x, ns)]
        between()
        x = [xi - _pair_dot(xci, xi) for xi, xci in zip(x, xc)]
        between()
        shift += 1
    return x


def _rwkv_kernel(z_ref, kk_ref, ka_ref, rk_ref, w0_ref, wdu_ref, a0_ref, wau_ref,
                 wgu_ref, lnw_ref, lnb_ref, wproj_ref, out_ref, state_ref, y_scr,
                 *staging, width, nblk):
    step = pl.program_id(0)
    nstage = len(staging) // 2

    @pl.when(step == 0)
    def _():
        for ref in (state_ref,) + tuple(staging):
            ref[...] = jnp.zeros_like(ref)

    for parity in range(2):
        @pl.when((step & 1) == parity)
        def _(parity=parity):
            _rwkv_step(z_ref, kk_ref, ka_ref, rk_ref, w0_ref, wdu_ref, a0_ref, wau_ref, wgu_ref,
                       lnw_ref, lnb_ref, wproj_ref, out_ref, state_ref, y_scr,
                       staging[parity * nstage:(parity + 1) * nstage],
                       staging[(1 - parity) * nstage:(2 - parity) * nstage], step, width, nblk)


def _rwkv_step(z_ref, kk_ref, ka_ref, rk_ref, w0_ref, wdu_ref, a0_ref, wau_ref, wgu_ref,
               lnw_ref, lnb_ref, wproj_ref, out_ref, state_ref, y_scr, cur, prev, step, width, nblk):
    lhs1_w, rhs1_w, lhs2_w, rt_w, v_w, glast_w, g_w, bonus_w = cur
    lhs1_s, rhs1_s, lhs2_s, rt_s, v_s, glast_s, g_s, bonus_s = prev
    rows = z_ref.shape[0]
    n = RWKV_HEAD
    L = CHUNK
    nchunks = rows // L
    W = width
    PW = 2 * n
    npairs = W // PW
    psl = lambda p: slice(p * PW, (p + 1) * PW)
    csl = lambda c: slice(c * L, (c + 1) * L)

    t = {}

    def task_r():
        t['r'] = z_ref[:, 0:W]

    def task_k():
        t['k'] = z_ref[:, W:2 * W]

    def task_v():
        t['v'] = z_ref[:, 2 * W:3 * W]
        v_w[...] = t['v']

    def task_lora():
        t['zs'] = z_ref[:, 3 * W:]
        gd = t['zs'][:, DECAY_LORA + AAA_LORA:]
        g_w[...] = _dotb(_sigmoid(gd), wgu_ref[...])

    def task_lw():
        wd = t['zs'][:, :DECAY_LORA]
        t['lw'] = -DECAY_SCALE * _sigmoid(w0_ref[...] + _dotb(jnp.tanh(wd), wdu_ref[...]))

    def task_a():
        ad = t['zs'][:, DECAY_LORA:DECAY_LORA + AAA_LORA]
        t['a'] = _sigmoid(a0_ref[...] + _dotb(ad, wau_ref[...]))

    def task_kk():
        t['kk'] = t['k'] * kk_ref[...]
        t['k_mod'] = t['k'] * (1.0 + (t['a'] - 1.0) * ka_ref[...])

    def task_norm():
        kk_sq = _segment_sums([t['kk'] * t['kk']], n)[0]
        t['kk'] = t['kk'] / jnp.maximum(jnp.sqrt(kk_sq), 1e-12)
        t['b'] = t['kk'] * t['a']

    def task_bonus():
        rk_sum = _segment_sums([t['r'] * t['k_mod'] * rk_ref[...]], n)[0]
        bonus_w[...] = rk_sum * t['v']

    tri_bf = jnp.where(_iota2((L, L), 0) >= _iota2((L, L), 1), 1.0, 0.0).astype(BF16)

    def task_decay(c):
        def run():
            sl = csl(c)
            lw_c = t['lw'][sl]
            cum = _dot_exact(tri_bf, lw_c, 2, m_left=True)
            cum_last = cum[L - 1:L, :]
            t['g_inv', c] = jnp.exp(-cum)
            t['glast', c] = jnp.exp(cum_last)
            glast_w[c] = t['glast', c]
            rt = t['r'][sl] * jnp.exp(cum)
            at = t['kk'][sl] * jnp.exp(cum - lw_c)
            rt_w[sl] = rt
            lhs1_w[c] = jnp.concatenate([at, rt], axis=0).astype(BF16)
        return run

    def dup_t(t1, t2):
        return jnp.concatenate([t1, t1, t2, t2], axis=0).T

    copy_is_head = (_iota2((W, 4 * L), 0) >> HEAD_SHIFT & 1) == (_iota2((W, 4 * L), 1) >> HEAD_SHIFT & 1)

    def task_rhs1(c):
        def run():
            sl = csl(c)
            t['bk', c] = dup_t(t['b'][sl] * t['g_inv', c], t['k_mod'][sl] * t['g_inv', c])
            bk = t['bk', c].astype(BF16)
            rhs1_w[c] = jnp.where(copy_is_head, bk, jnp.zeros_like(bk))
        return run

    first_copy = (_iota2((n, 2 * PW), 1) >> HEAD_SHIFT & 1) == 0

    def task_lhs2(c):
        def run():
            bk = t['bk', c] * jnp.broadcast_to(t['glast', c], (8, W)).T[:, 0:1]
            for p in range(npairs):
                lhs2_w[c, :, p * 2 * PW:(p + 1) * 2 * PW] = jnp.where(
                    first_copy, bk[p * PW:p * PW + n], bk[p * PW + n:(p + 1) * PW]).astype(BF16)
        return run

    tasks = [task_r, task_k, task_v, task_lora, task_lw, task_a, task_kk, task_norm, task_bonus]
    for c in range(nchunks):
        tasks += [task_decay(c), task_rhs1(c), task_lhs2(c)]
    tasks = iter(tasks)

    def tick(k=1):
        for _ in range(k):
            task = next(tasks, None)
            if task is not None:
                task()

    ri = _iota2((L, 2 * PW), 0)
    ci = _iota2((L, 2 * PW), 1) & (L - 1)
    strict = (ri > ci)[:, :PW]
    incl = ri >= ci
    eye_cat = (_iota2((n, PW), 0) == (_iota2((n, PW), 1) & (n - 1))).astype(F32)
    zeros_bd = jnp.zeros((PW, PW), BF16)
    units = [(c, p) for c in range(nchunks) for p in range(npairs)]
    x1 = [jnp.dot(lhs1_s[c, :, psl(p)], rhs1_s[c, psl(p), :], preferred_element_type=F32)
          for c, p in units]
    n_ab = [m[:L, :PW] for m in x1]
    n_ak = [jnp.where(strict, m[:L, PW:], 0.0).astype(BF16) for m in x1]
    m_rbk = [jnp.where(incl, m[L:, :], 0.0).astype(BF16) for m in x1]
    tick(3)
    t_inv = _unit_lower_inverse(n_ab, between=tick)
    vbd = [_pair_blockdiag(v_s[csl(c), psl(p)]) for c, p in units]
    nv = [_pair_dot(a, b_bd=vb) for a, vb in zip(n_ak, vbd)]
    tick()
    pq = [jnp.dot(tm.astype(BF16),
                  jnp.concatenate([_pair_blockdiag(lhs1_s[c, :L, psl(p)]), _pair_blockdiag(-q)], axis=1),
                  preferred_element_type=F32) for tm, q, (c, p) in zip(t_inv, nv, units)]
    tick()
    r2 = []
    for m, pqi, vb, (c, p) in zip(m_rbk, pq, vbd, units):
        rhs2 = jnp.concatenate(
            [jnp.concatenate([_pair_blockdiag(pqi[:, :PW]), _pair_blockdiag(pqi[:, PW:])], axis=1),
             jnp.concatenate([zeros_bd, vb], axis=1)], axis=0)
        l2 = jnp.concatenate([m, lhs2_s[c, :, p * 2 * PW:(p + 1) * 2 * PW]], axis=0)
        r2.append(jnp.dot(l2, rhs2, preferred_element_type=F32))
    tick()
    lhs3 = [jnp.concatenate([rt_s[csl(c), psl(p)] - m[:L, :PW],
                             eye_cat * glast_s[c][:, psl(p)] - m[L:, :PW]], axis=0)
            for m, (c, p) in zip(r2, units)]

    first = lax.rem(step + nblk - 1, nblk) == 0
    state = [jnp.where(first, 0.0, state_ref[p]) for p in range(npairs)]
    for c in range(nchunks):
        r3 = [_pair_dot(lhs3[c * npairs + p], state[p]) for p in range(npairs)]
        for p in range(npairs):
            m = r2[c * npairs + p]
            y_scr[csl(c), psl(p)] = r3[p][:L] + m[:L, PW:]
            state[p] = r3[p][L:] + m[L:, PW:]
        tick()
    for p in range(npairs):
        state_ref[p] = state[p]
    tick(len(units))

    y = y_scr[...]
    inv_n = 1.0 / n
    yc = y - _segment_sums([y], n)[0] * inv_n
    var = _segment_sums([yc * yc], n)[0] * inv_n
    yn = yc * lax.rsqrt(var + LNX_EPS) * lnw_ref[...] + lnb_ref[...]
    out_ref[...] = _dotb((yn + bonus_s[...]) * g_s[...], wproj_ref[...])


def _rwkv_mix(z_rwkv, seq, layer, k_k, k_a, r_k, w0, wdu, a0, wau, wgu, lnw, lnb, wproj, rows=512):
    t, ncols = z_rwkv.shape
    W = k_k.shape[-1]
    heads = W // RWKV_HEAD
    d = wproj.shape[-1]
    nblk = seq // rows
    nsteps = t // rows
    nchunks = rows // CHUNK
    full = lambda arr: _layer_spec(arr, layer)
    params = (k_k, k_a, r_k, w0, wdu, a0, wau, wgu, lnw, lnb, wproj)
    return pl.pallas_call(
        functools.partial(_rwkv_kernel, width=W, nblk=nblk),
        grid=(nsteps + 1,),
        in_specs=[pl.BlockSpec((rows, ncols), lambda s: (jnp.minimum(s, nsteps - 1), 0))]
        + [full(p) for p in params],
        out_specs=pl.BlockSpec((rows, d), lambda s: (jnp.maximum(s - 1, 0), 0)),
        out_shape=jax.ShapeDtypeStruct((t, d), F32),
        scratch_shapes=[pltpu.VMEM((heads // 2, RWKV_HEAD, 2 * RWKV_HEAD), F32),
                        pltpu.VMEM((rows, W), F32)] + 2 * [
                        pltpu.VMEM((nchunks, 2 * CHUNK, W), BF16),
                        pltpu.VMEM((nchunks, W, 4 * CHUNK), BF16),
                        pltpu.VMEM((nchunks, RWKV_HEAD, 2 * W), BF16),
                        pltpu.VMEM((rows, W), F32),
                        pltpu.VMEM((rows, W), F32),
                        pltpu.VMEM((nchunks, 1, W), F32),
                        pltpu.VMEM((rows, W), F32),
                        pltpu.VMEM((rows, W), F32)],
        compiler_params=pltpu.CompilerParams(dimension_semantics=("arbitrary",),
                                             vmem_limit_bytes=VMEM_LIMIT),
        name="rwkv_mix",
    )(z_rwkv, *params)


def _s5_disc_kernel(are_ref, aim_ref, ldt_ref, bre_ref, bim_ref, abr_ref, abi_ref, bbr_ref, bbi_ref):
    dt = jnp.exp(ldt_ref[...])
    are = jnp.minimum(are_ref[...], -1e-4)
    aim = aim_ref[...]
    mag = jnp.exp(dt * are)
    abr = mag * jnp.cos(dt * aim)
    abi = mag * jnp.sin(dt * aim)
    den = are * are + aim * aim
    nr = abr - 1.0
    cre = (nr * are + abi * aim) / den
    cim = (abi * are - nr * aim) / den
    abr_ref[...] = abr
    abi_ref[...] = abi
    br = bre_ref[...]
    bi = bim_ref[...]
    bbr_ref[...] = cre[None] * br - cim[None] * bi
    bbi_ref[...] = cre[None] * bi + cim[None] * br


def _s5_disc(a_re, a_im, log_dt, b_re_c, b_im_c):
    G, P = a_re.shape
    C = b_re_c.shape[0]
    return pl.pallas_call(
        _s5_disc_kernel,
        out_shape=[jax.ShapeDtypeStruct((G, P), F32), jax.ShapeDtypeStruct((G, P), F32),
                   jax.ShapeDtypeStruct((C, G, P), F32), jax.ShapeDtypeStruct((C, G, P), F32)],
        name="s5_disc",
    )(a_re, a_im, log_dt.reshape(G, 1), b_re_c, b_im_c)


def _s5_kernel(u_ref, bmat_ref, cmat_ref, abr_ref, abi_ref, dskip_ref, wglu_ref, bglu_ref,
               out_ref, st_r_ref, st_i_ref, x_scr, bt_scr, tb_scr, *, nslab):
    _, nb, rt, _ = u_ref.shape
    rows = nb * rt
    ns = abr_ref.shape[1]
    sw = ns // nslab
    d = out_ref.shape[2]

    @pl.when(pl.program_id(1) == 0)
    def _():
        st_r_ref[...] = jnp.zeros_like(st_r_ref)
        st_i_ref[...] = jnp.zeros_like(st_i_ref)

    bt_scr[...] = u_ref[...].reshape(nslab, rows, LANES)
    def input_drive(s):
        for t in range(rt):
            tb_scr[s, t * nb:(t + 1) * nb, :] = bt_scr[s, pl.ds(t, nb, stride=rt), :]
        return _dotb(tb_scr[s], bmat_ref[s])

    ys = []
    bu_next = input_drive(0)
    for s in range(nslab):
        bu = bu_next
        if s + 1 < nslab:
            bu_next = input_drive(s + 1)
        cs = slice(s * sw, (s + 1) * sw)
        ar = abr_ref[:, cs]
        ai = abi_ref[:, cs]
        xr = st_r_ref[:, cs]
        xi = st_i_ref[:, cs]
        for t in range(rt):
            ts = slice(t * nb, (t + 1) * nb)
            xr, xi = ar * xr - ai * xi + bu[ts, :sw], ar * xi + ai * xr + bu[ts, sw:]
            x_scr[ts, :sw] = xr
            x_scr[ts, sw:] = xi
        st_r_ref[:, cs] = xr
        st_i_ref[:, cs] = xi
        tb_scr[s] = _dotb(x_scr[...], cmat_ref[s])
        y_s = jnp.concatenate([tb_scr[s, pl.ds(b, rt, stride=nb), :] for b in range(nb)], axis=0)
        ys.append(y_s + dskip_ref[:, s * LANES:(s + 1) * LANES] * bt_scr[s])
    y = jnp.concatenate(ys, axis=-1)
    y = 0.5 * y * (1.0 + lax.erf(y * (1.0 / math.sqrt(2.0))))
    zz = _dotb(y, wglu_ref[...]) + bglu_ref[...]
    out_ref[...] = (zz[:, :d] * _sigmoid(zz[:, d:])).reshape(nb, rt, d)


def _s5_mix(u4, layer, bmat, cmat, abr, abi, dskip, wglu, bglu, nb=8, rt=128):
    nslab, bsz, seq, _ = u4.shape
    assert rt % 8 == 0 and bmat.shape[1] == nslab
    ns = abr.shape[-1]
    d = wglu.shape[-1] // 2
    full = lambda arr: _layer_spec(arr, layer)
    params = (bmat, cmat, abr, abi, dskip, wglu, bglu)
    return pl.pallas_call(
        functools.partial(_s5_kernel, nslab=nslab),
        grid=(bsz // nb, seq // rt),
        in_specs=[pl.BlockSpec((nslab, nb, rt, LANES), lambda b, j: (0, b, j, 0))] + [full(p) for p in params],
        out_specs=pl.BlockSpec((nb, rt, d), lambda b, j: (b, j, 0)),
        out_shape=jax.ShapeDtypeStruct((bsz, seq, d), F32),
        scratch_shapes=[pltpu.VMEM((nb, ns), F32), pltpu.VMEM((nb, ns), F32),
                        pltpu.VMEM((nb * rt, 2 * ns // nslab), F32),
                        pltpu.VMEM((nslab, nb * rt, LANES), F32),
                        pltpu.VMEM((nslab, nb * rt, LANES), F32)],
        compiler_params=pltpu.CompilerParams(dimension_semantics=("parallel", "arbitrary"),
                                             vmem_limit_bytes=VMEM_LIMIT),
        name="s5_mix",
    )(u4, *params)


def _merge_ffn_kernel(x_ref, gt_ref, ya_ref, yb_ref, wout_ref, g2_ref, wup_ref, wdn_ref, gf_ref,
                      out_ref, *, final_norm, ff_chunk):
    d = x_ref.shape[1]
    gt = gt_ref[...]
    m = gt[:, :d] * ya_ref[...] + gt[:, d:] * yb_ref[...]
    x1 = x_ref[...] + _dotb(m, wout_ref[...])
    xn = x1 * lax.rsqrt(jnp.mean(x1 * x1, axis=-1, keepdims=True) + NORM_EPS) * g2_ref[...]
    xn = xn.astype(BF16)
    acc = x1
    dff = wup_ref.shape[1]
    for c in range(dff // ff_chunk):
        cs = slice(c * ff_chunk, (c + 1) * ff_chunk)
        h = jnp.dot(xn, wup_ref[:, cs], preferred_element_type=F32)
        h = jnp.square(jnp.maximum(h, 0.0))
        acc = acc + _dotb(h, wdn_ref[cs, :])
    if final_norm:
        acc = acc * lax.rsqrt(jnp.mean(acc * acc, axis=-1, keepdims=True) + NORM_EPS) * gf_ref[...]
    out_ref[...] = acc


def _merge_ffn(x2, gates, ya, yb, layer, wout, g2, wup, wdn, gf, final_norm, tm=512, ff_chunk=1024):
    t, d = x2.shape
    row = lambda w: pl.BlockSpec((tm, w), lambda i: (i, 0))
    full = lambda arr: _layer_spec(arr, layer)
    return pl.pallas_call(
        functools.partial(_merge_ffn_kernel, final_norm=final_norm, ff_chunk=ff_chunk),
        grid=(t // tm,),
        in_specs=[row(d), row(2 * d), row(d), row(d), full(wout), full(g2), full(wup), full(wdn),
                  pl.BlockSpec(gf.shape, lambda i: (0, 0))],
        out_specs=row(d),
        out_shape=jax.ShapeDtypeStruct((t, d), F32),
        compiler_params=pltpu.CompilerParams(dimension_semantics=("parallel",),
                                             vmem_limit_bytes=VMEM_LIMIT),
        name="merge_ffn",
    )(x2, gates, ya, yb, wout, g2, wup, wdn, gf)


def _s5_layer_params(a_re, a_im, log_dt, b_re, b_im, c_re, c_im):
    G, P, C = b_re.shape
    gps = LANES // C
    nslab = G // gps
    abr, abi, bbr, bbi = _s5_disc(a_re, a_im, log_dt, jnp.transpose(b_re, (2, 0, 1)),
                                  jnp.transpose(b_im, (2, 0, 1)))
    eye = jnp.eye(gps, dtype=F32)

    def bd_in(bb):
        bb = jnp.transpose(bb, (1, 0, 2)).reshape(nslab, gps, C, P)
        return jnp.einsum('sgcp,gh->sgchp', bb, eye).reshape(nslab, gps * C, gps * P)

    def bd_out(cc):
        cc = cc.reshape(nslab, gps, C, P)
        return jnp.einsum('sgcp,gh->sgphc', cc, eye).reshape(nslab, gps * P, gps * C)

    bmat = jnp.concatenate([bd_in(bbr), bd_in(bbi)], axis=2).astype(BF16)
    cmat = jnp.concatenate([bd_out(c_re), -bd_out(c_im)], axis=1).astype(BF16)
    return bmat, cmat, abr.reshape(1, G * P), abi.reshape(1, G * P)


def kernel(x, norm1_g, w_in, mu_shift, k_k, k_a, r_k, w0, w_decay_up, a0, w_aaa_up, w_gate_up,
           lnx_w, lnx_b, w_rwkv_proj, a_re, a_im, log_dt, b_re, b_im, c_re, c_im, d_skip, w_glu,
           b_glu, w_out, norm2_g, w_ff_up, w_ff_down, norm_f_g):
    bsz, seq, d = x.shape
    depth = w_in.shape[0]
    n_shift = mu_shift.shape[1]
    n_ssm = d_skip.shape[1]
    t = bsz * seq
    vec = lambda a: a.reshape(a.shape[0], 1, -1)
    bf = lambda a: a.astype(BF16)
    rwkv_params = (vec(k_k), vec(k_a), vec(r_k), vec(w0), bf(w_decay_up), vec(a0),
                   bf(w_aaa_up), bf(w_gate_up), vec(lnx_w), vec(lnx_b), bf(w_rwkv_proj))
    s5_params = jax.vmap(_s5_layer_params)(a_re, a_im, log_dt, b_re, b_im, c_re, c_im)
    s5_params += (vec(d_skip), bf(w_glu), vec(b_glu))
    w_in_bf, w_out_bf, w_up_bf, w_dn_bf = bf(w_in), bf(w_out), bf(w_ff_up), bf(w_ff_down)
    g1, g2, mu = vec(norm1_g), vec(norm2_g), vec(mu_shift)
    x2 = x.reshape(t, d)
    for l in range(depth):
        z_rwkv, u, gates = _in_proj(x2, seq, g1, w_in_bf, mu, l, n_shift, n_ssm)
        y_a = _rwkv_mix(z_rwkv, seq, l, *rwkv_params)
        y_b = _s5_mix(u.reshape(-1, bsz, seq, LANES), l, *s5_params)
        x2 = _merge_ffn(x2, gates, y_a, y_b.reshape(t, d), l, w_out_bf, g2, w_up_bf,
                        w_dn_bf, norm_f_g.reshape(1, -1), final_norm=(l == depth - 1))
    return x2.reshape(bsz, seq, d)
```

```python
import functools
import math

import jax
import jax.numpy as jnp
from jax import lax
from jax.experimental import pallas as pl
from jax.experimental.pallas import tpu as pltpu

F32 = jnp.float32
BF16 = jnp.bfloat16

RWKV_HEAD = 64
DECAY_LORA = 64
AAA_LORA = 64
LNX_EPS = 64e-5
DECAY_SCALE = math.exp(-0.5)
NORM_EPS = 1e-6

CHUNK = 64
HEAD_SHIFT = RWKV_HEAD.bit_length() - 1
LANES = 128
MXU_TILE = 256
VMEM_LIMIT = 56 * 1024 * 1024


def _dotb(a, b):
    return jnp.dot(a.astype(BF16), b.astype(BF16), preferred_element_type=F32)


def _sigmoid(x):
    return 0.5 * jnp.tanh(0.5 * x) + 0.5


def _iota2(shape, axis):
    return lax.broadcasted_iota(jnp.int32, shape, axis)


def _layer_spec(arr, layer):
    return pl.BlockSpec((None,) + arr.shape[1:], lambda *_: (layer,) + (0,) * (arr.ndim - 1),
                        pipeline_mode=pl.Buffered(1))


def _in_proj_kernel(x_ref, g_ref, w_ref, mu_ref, zr_ref, u_ref, gt_ref, carry_ref, *, n_shift, n_ssm, nblk):
    @pl.when(pl.program_id(0) == 0)
    def _():
        carry_ref[...] = jnp.zeros_like(carry_ref)

    x = x_ref[...]
    xn = x * lax.rsqrt(jnp.mean(x * x, axis=-1, keepdims=True) + NORM_EPS) * g_ref[...]
    z = _dotb(xn, w_ref[...])
    zr = z[:, :n_shift]
    rows = zr.shape[0]
    seq_start = lax.rem(pl.program_id(0), nblk) == 0
    carry = jnp.where(seq_start, 0.0, carry_ref[0:1, :])
    zprev = jnp.where(_iota2((rows, 1), 0) == 0, carry, pltpu.roll(zr, 1, axis=0))
    carry_ref[0:1, :] = zr[rows - 1:rows, :]
    zr_ref[...] = zr + (zprev - zr) * mu_ref[...]
    for s in range(u_ref.shape[0]):
        u_ref[s] = z[:, n_shift + s * LANES:n_shift + (s + 1) * LANES]
    gt_ref[...] = _sigmoid(z[:, n_shift + n_ssm:])


def _in_proj(x2, seq, g, w_bf, mu, layer, n_shift, n_ssm, tm=512):
    t, d = x2.shape
    n_in = w_bf.shape[-1]
    n_gate = n_in - n_shift - n_ssm
    return pl.pallas_call(
        functools.partial(_in_proj_kernel, n_shift=n_shift, n_ssm=n_ssm, nblk=seq // tm),
        grid=(t // tm,),
        in_specs=[pl.BlockSpec((tm, d), lambda i: (i, 0)),
                  _layer_spec(g, layer), _layer_spec(w_bf, layer), _layer_spec(mu, layer)],
        out_specs=[pl.BlockSpec((tm, n_shift), lambda i: (i, 0)),
                   pl.BlockSpec((n_ssm // LANES, tm, LANES), lambda i: (0, i, 0)),
                   pl.BlockSpec((tm, n_gate), lambda i: (i, 0))],
        out_shape=[jax.ShapeDtypeStruct((t, n_shift), F32),
                   jax.ShapeDtypeStruct((n_ssm // LANES, t, LANES), F32),
                   jax.ShapeDtypeStruct((t, n_gate), F32)],
        scratch_shapes=[pltpu.VMEM((8, n_shift), F32)],
        compiler_params=pltpu.CompilerParams(dimension_semantics=("arbitrary",),
                                             vmem_limit_bytes=VMEM_LIMIT),
        name="in_proj",
    )(x2, g, w_bf, mu)


def _dot_exact(m_bf, x, terms, *, m_left):
    acc = None
    rem = x
    for _ in range(terms):
        piece = rem.astype(BF16)
        d = (jnp.dot(m_bf, piece, preferred_element_type=F32) if m_left
             else jnp.dot(piece, m_bf, preferred_element_type=F32))
        acc = d if acc is None else acc + d
        rem = rem - piece.astype(F32)
    return acc


def _segment_sums(ts, n):
    rows, width = ts[0].shape
    tile = MXU_TILE
    ncol = width // tile
    lg = n.bit_length() - 1
    seg = jnp.where((_iota2((tile, tile), 0) >> lg) == (_iota2((tile, tile), 1) >> lg), 1.0, 0.0).astype(BF16)
    pieces = []
    for t in ts:
        hi = t.astype(BF16)
        pieces += [hi[:, j * tile:(j + 1) * tile] for j in range(ncol)]
    res = jnp.dot(jnp.concatenate(pieces, axis=0), seg, preferred_element_type=F32)
    blk = lambda i: res[i * rows:(i + 1) * rows]
    return [jnp.concatenate([blk(a * ncol + j) for j in range(ncol)], axis=1)
            for a in range(len(ts))]


def _pair_blockdiag(m_cat, mask=None):
    m2 = jnp.concatenate([m_cat, m_cat], axis=0)
    keep = (_iota2(m2.shape, 0) >> HEAD_SHIFT) == (_iota2(m2.shape, 1) >> HEAD_SHIFT)
    if mask is not None:
        keep = keep & jnp.concatenate([mask, mask], axis=0)
    return jnp.where(keep, m2, jnp.zeros_like(m2)).astype(BF16)


def _pair_dot(a_cat, b_cat=None, b_bd=None, mask=None):
    b_bd = _pair_blockdiag(b_cat, mask) if b_bd is None else b_bd
    return jnp.dot(a_cat.astype(BF16), b_bd, preferred_element_type=F32)


def _unit_lower_inverse(ns, between=lambda: None):
    L, lanes = ns[0].shape
    ri = _iota2((L, lanes), 0)
    ci = _iota2((L, lanes), 1) & (L - 1)
    eye = (ri == ci).astype(F32)
    same8 = ((ri >> 3) == (ci >> 3)) & (ri > ci)
    n0 = [jnp.where(same8, m, 0.0) for m in ns]
    x = [eye - m for m in n0]
    p = [_pair_dot(m0, m, mask=same8) for m0, m in zip(n0, ns)]
    between()
    pbd = [_pair_blockdiag(pi) for pi in p]
    x = [xi + _pair_dot(xi, b_bd=pi) for xi, pi in zip(x, pbd)]
    between()
    p = [_pair_dot(pi, b_bd=qi) for pi, qi in zip(p, pbd)]
    between()
    x = [xi + _pair_dot(xi, pi) for xi, pi in zip(x, p)]
    between()
    shift = 4
    while (1 << shift) <= L:
        sel = ((ri >> shift) == (ci >> shift)) & ((ri >> (shift - 1)) > (ci >> (shift - 1)))
        xc = [_pair_dot(xi, m, mask=sel).astype(BF16) for xi, m in zip(x, ns)]
        between()
        x = [xi - _pair_dot(xci, xi) for xi, xci in zip(x, xc)]
        between()
        shift += 1
    return x


def _rwkv_kernel(z_ref, kk_ref, ka_ref, rk_ref, w0_ref, wdu_ref, a0_ref, wau_ref,
                 wgu_ref, lnw_ref, lnb_ref, wproj_ref, out_ref, state_ref, y_scr,
                 *staging, width, nblk):
    step = pl.program_id(0)
    nstage = len(staging) // 2

    @pl.when(step == 0)
    def _():
        for ref in (state_ref,) + tuple(staging):
            ref[...] = jnp.zeros_like(ref)

    for parity in range(2):
        @pl.when((step & 1) == parity)
        def _(parity=parity):
            _rwkv_step(z_ref, kk_ref, ka_ref, rk_ref, w0_ref, wdu_ref, a0_ref, wau_ref, wgu_ref,
                       lnw_ref, lnb_ref, wproj_ref, out_ref, state_ref, y_scr,
                       staging[parity * nstage:(parity + 1) * nstage],
                       staging[(1 - parity) * nstage:(2 - parity) * nstage], step, width, nblk)


def _rwkv_step(z_ref, kk_ref, ka_ref, rk_ref, w0_ref, wdu_ref, a0_ref, wau_ref, wgu_ref,
               lnw_ref, lnb_ref, wproj_ref, out_ref, state_ref, y_scr, cur, prev, step, width, nblk):
    lhs1_w, rhs1_w, lhs2_w, rt_w, v_w, glast_w, g_w, bonus_w = cur
    lhs1_s, rhs1_s, lhs2_s, rt_s, v_s, glast_s, g_s, bonus_s = prev
    rows = z_ref.shape[0]
    n = RWKV_HEAD
    L = CHUNK
    nchunks = rows // L
    W = width
    PW = 2 * n
    npairs = W // PW
    psl = lambda p: slice(p * PW, (p + 1) * PW)
    csl = lambda c: slice(c * L, (c + 1) * L)

    t = {}

    def task_r():
        t['r'] = z_ref[:, 0:W]

    def task_k():
        t['k'] = z_ref[:, W:2 * W]

    def task_v():
        t['v'] = z_ref[:, 2 * W:3 * W]
        v_w[...] = t['v']

    def task_lora():
        t['zs'] = z_ref[:, 3 * W:]
        gd = t['zs'][:, DECAY_LORA + AAA_LORA:]
        g_w[...] = _dotb(_sigmoid(gd), wgu_ref[...])

    def task_lw():
        wd = t['zs'][:, :DECAY_LORA]
        t['lw'] = -DECAY_SCALE * _sigmoid(w0_ref[...] + _dotb(jnp.tanh(wd), wdu_ref[...]))

    def task_a():
        ad = t['zs'][:, DECAY_LORA:DECAY_LORA + AAA_LORA]
        t['a'] = _sigmoid(a0_ref[...] + _dotb(ad, wau_ref[...]))

    def task_kk():
        t['kk'] = t['k'] * kk_ref[...]
        t['k_mod'] = t['k'] * (1.0 + (t['a'] - 1.0) * ka_ref[...])

    def task_norm():
        kk_sq = _segment_sums([t['kk'] * t['kk']], n)[0]
        t['kk'] = t['kk'] / jnp.maximum(jnp.sqrt(kk_sq), 1e-12)
        t['b'] = t['kk'] * t['a']

    def task_bonus():
        rk_sum = _segment_sums([t['r'] * t['k_mod'] * rk_ref[...]], n)[0]
        bonus_w[...] = rk_sum * t['v']

    tri_bf = jnp.where(_iota2((L, L), 0) >= _iota2((L, L), 1), 1.0, 0.0).astype(BF16)

    def task_decay(c):
        def run():
            sl = csl(c)
            lw_c = t['lw'][sl]
            cum = _dot_exact(tri_bf, lw_c, 2, m_left=True)
            cum_last = cum[L - 1:L, :]
            t['g_inv', c] = jnp.exp(-cum)
            t['glast', c] = jnp.exp(cum_last)
            glast_w[c] = t['glast', c]
            rt = t['r'][sl] * jnp.exp(cum)
            at = t['kk'][sl] * jnp.exp(cum - lw_c)
            rt_w[sl] = rt
            lhs1_w[c] = jnp.concatenate([at, rt], axis=0).astype(BF16)
        return run

    def dup_t(t1, t2):
        return jnp.concatenate([t1, t1, t2, t2], axis=0).T

    copy_is_head = (_iota2((W, 4 * L), 0) >> HEAD_SHIFT & 1) == (_iota2((W, 4 * L), 1) >> HEAD_SHIFT & 1)

    def task_rhs1(c):
        def run():
            sl = csl(c)
            t['bk', c] = dup_t(t['b'][sl] * t['g_inv', c], t['k_mod'][sl] * t['g_inv', c])
            bk = t['bk', c].astype(BF16)
            rhs1_w[c] = jnp.where(copy_is_head, bk, jnp.zeros_like(bk))
        return run

    first_copy = (_iota2((n, 2 * PW), 1) >> HEAD_SHIFT & 1) == 0

    def task_lhs2(c):
        def run():
            bk = t['bk', c] * jnp.broadcast_to(t['glast', c], (8, W)).T[:, 0:1]
            for p in range(npairs):
                lhs2_w[c, :, p * 2 * PW:(p + 1) * 2 * PW] = jnp.where(
                    first_copy, bk[p * PW:p * PW + n], bk[p * PW + n:(p + 1) * PW]).astype(BF16)
        return run

    tasks = [task_r, task_k, task_v, task_lora, task_lw, task_a, task_kk, task_norm, task_bonus]
    for c in range(nchunks):
        tasks += [task_decay(c), task_rhs1(c), task_lhs2(c)]
    tasks = iter(tasks)

    def tick(k=1):
        for _ in range(k):
            task = next(tasks, None)
            if task is not None:
                task()

    ri = _iota2((L, 2 * PW), 0)
    ci = _iota2((L, 2 * PW), 1) & (L - 1)
    strict = (ri > ci)[:, :PW]
    incl = ri >= ci
    eye_cat = (_iota2((n, PW), 0) == (_iota2((n, PW), 1) & (n - 1))).astype(F32)
    zeros_bd = jnp.zeros((PW, PW), BF16)
    units = [(c, p) for c in range(nchunks) for p in range(npairs)]
    x1 = [jnp.dot(lhs1_s[c, :, psl(p)], rhs1_s[c, psl(p), :], preferred_element_type=F32)
          for c, p in units]
    n_ab = [m[:L, :PW] for m in x1]
    n_ak = [jnp.where(strict, m[:L, PW:], 0.0).astype(BF16) for m in x1]
    m_rbk = [jnp.where(incl, m[L:, :], 0.0).astype(BF16) for m in x1]
    tick(5)
    t_inv = _unit_lower_inverse(n_ab, between=tick)
    vbd = [_pair_blockdiag(v_s[csl(c), psl(p)]) for c, p in units]
    nv = [_pair_dot(a, b_bd=vb) for a, vb in zip(n_ak, vbd)]
    tick()
    pq = [jnp.dot(tm.astype(BF16),
                  jnp.concatenate([_pair_blockdiag(lhs1_s[c, :L, psl(p)]), _pair_blockdiag(-q)], axis=1),
                  preferred_element_type=F32) for tm, q, (c, p) in zip(t_inv, nv, units)]
    tick()
    r2 = []
    for m, pqi, vb, (c, p) in zip(m_rbk, pq, vbd, units):
        rhs2 = jnp.concatenate(
            [jnp.concatenate([_pair_blockdiag(pqi[:, :PW]), _pair_blockdiag(pqi[:, PW:])], axis=1),
             jnp.concatenate([zeros_bd, vb], axis=1)], axis=0)
        l2 = jnp.concatenate([m, lhs2_s[c, :, p * 2 * PW:(p + 1) * 2 * PW]], axis=0)
        r2.append(jnp.dot(l2, rhs2, preferred_element_type=F32))
    tick()
    lhs3 = [jnp.concatenate([rt_s[csl(c), psl(p)] - m[:L, :PW],
                             eye_cat * glast_s[c][:, psl(p)] - m[L:, :PW]], axis=0)
            for m, (c, p) in zip(r2, units)]

    first = lax.rem(step + nblk - 1, nblk) == 0
    state = [jnp.where(first, 0.0, state_ref[p]) for p in range(npairs)]
    for c in range(nchunks):
        r3 = [_pair_dot(lhs3[c * npairs + p], state[p]) for p in range(npairs)]
        for p in range(npairs):
            m = r2[c * npairs + p]
            y_scr[csl(c), psl(p)] = r3[p][:L] + m[:L, PW:]
            state[p] = r3[p][L:] + m[L:, PW:]
        tick(2)
    for p in range(npairs):
        state_ref[p] = state[p]
    tick(len(units))

    y = y_scr[...]
    inv_n = 1.0 / n
    yc = y - _segment_sums([y], n)[0] * inv_n
    var = _segment_sums([yc * yc], n)[0] * inv_n
    yn = yc * lax.rsqrt(var + LNX_EPS) * lnw_ref[...] + lnb_ref[...]
    out_ref[...] = _dotb((yn + bonus_s[...]) * g_s[...], wproj_ref[...])


def _rwkv_mix(z_rwkv, seq, layer, k_k, k_a, r_k, w0, wdu, a0, wau, wgu, lnw, lnb, wproj, rows=512):
    t, ncols = z_rwkv.shape
    W = k_k.shape[-1]
    heads = W // RWKV_HEAD
    d = wproj.shape[-1]
    nblk = seq // rows
    nsteps = t // rows
    nchunks = rows // CHUNK
    full = lambda arr: _layer_spec(arr, layer)
    params = (k_k, k_a, r_k, w0, wdu, a0, wau, wgu, lnw, lnb, wproj)
    return pl.pallas_call(
        functools.partial(_rwkv_kernel, width=W, nblk=nblk),
        grid=(nsteps + 1,),
        in_specs=[pl.BlockSpec((rows, ncols), lambda s: (jnp.minimum(s, nsteps - 1), 0))]
        + [full(p) for p in params],
        out_specs=pl.BlockSpec((rows, d), lambda s: (jnp.maximum(s - 1, 0), 0)),
        out_shape=jax.ShapeDtypeStruct((t, d), F32),
        scratch_shapes=[pltpu.VMEM((heads // 2, RWKV_HEAD, 2 * RWKV_HEAD), F32),
                        pltpu.VMEM((rows, W), F32)] + 2 * [
                        pltpu.VMEM((nchunks, 2 * CHUNK, W), BF16),
                        pltpu.VMEM((nchunks, W, 4 * CHUNK), BF16),
                        pltpu.VMEM((nchunks, RWKV_HEAD, 2 * W), BF16),
                        pltpu.VMEM((rows, W), F32),
                        pltpu.VMEM((rows, W), F32),
                        pltpu.VMEM((nchunks, 1, W), F32),
                        pltpu.VMEM((rows, W), F32),
                        pltpu.VMEM((rows, W), F32)],
        compiler_params=pltpu.CompilerParams(dimension_semantics=("arbitrary",),
                                             vmem_limit_bytes=VMEM_LIMIT),
        name="rwkv_mix",
    )(z_rwkv, *params)


def _s5_disc_kernel(are_ref, aim_ref, ldt_ref, bre_ref, bim_ref, abr_ref, abi_ref, bbr_ref, bbi_ref):
    dt = jnp.exp(ldt_ref[...])
    are = jnp.minimum(are_ref[...], -1e-4)
    aim = aim_ref[...]
    mag = jnp.exp(dt * are)
    abr = mag * jnp.cos(dt * aim)
    abi = mag * jnp.sin(dt * aim)
    den = are * are + aim * aim
    nr = abr - 1.0
    cre = (nr * are + abi * aim) / den
    cim = (abi * are - nr * aim) / den
    abr_ref[...] = abr
    abi_ref[...] = abi
    br = bre_ref[...]
    bi = bim_ref[...]
    bbr_ref[...] = cre[None] * br - cim[None] * bi
    bbi_ref[...] = cre[None] * bi + cim[None] * br


def _s5_disc(a_re, a_im, log_dt, b_re_c, b_im_c):
    G, P = a_re.shape
    C = b_re_c.shape[0]
    return pl.pallas_call(
        _s5_disc_kernel,
        out_shape=[jax.ShapeDtypeStruct((G, P), F32), jax.ShapeDtypeStruct((G, P), F32),
                   jax.ShapeDtypeStruct((C, G, P), F32), jax.ShapeDtypeStruct((C, G, P), F32)],
        name="s5_disc",
    )(a_re, a_im, log_dt.reshape(G, 1), b_re_c, b_im_c)


def _s5_kernel(u_ref, bmat_ref, cmat_ref, abr_ref, abi_ref, dskip_ref, wglu_ref, bglu_ref,
               out_ref, st_r_ref, st_i_ref, x_scr, bt_scr, tb_scr, *, nslab):
    _, nb, rt, _ = u_ref.shape
    rows = nb * rt
    ns = abr_ref.shape[1]
    sw = ns // nslab
    d = out_ref.shape[2]

    @pl.when(pl.program_id(1) == 0)
    def _():
        st_r_ref[...] = jnp.zeros_like(st_r_ref)
        st_i_ref[...] = jnp.zeros_like(st_i_ref)

    bt_scr[...] = u_ref[...].reshape(nslab, rows, LANES)
    def input_drive(s):
        for t in range(rt):
            tb_scr[s, t * nb:(t + 1) * nb, :] = bt_scr[s, pl.ds(t, nb, stride=rt), :]
        return _dotb(tb_scr[s], bmat_ref[s])

    ys = []
    bu_next = input_drive(0)
    for s in range(nslab):
        bu = bu_next
        if s + 1 < nslab:
            bu_next = input_drive(s + 1)
        cs = slice(s * sw, (s + 1) * sw)
        ar = abr_ref[:, cs]
        ai = abi_ref[:, cs]
        xr = st_r_ref[:, cs]
        xi = st_i_ref[:, cs]
        for t in range(rt):
            ts = slice(t * nb, (t + 1) * nb)
            xr, xi = ar * xr - ai * xi + bu[ts, :sw], ar * xi + ai * xr + bu[ts, sw:]
            x_scr[ts, :sw] = xr
            x_scr[ts, sw:] = xi
        st_r_ref[:, cs] = xr
        st_i_ref[:, cs] = xi
        tb_scr[s] = _dotb(x_scr[...], cmat_ref[s])
        y_s = jnp.concatenate([tb_scr[s, pl.ds(b, rt, stride=nb), :] for b in range(nb)], axis=0)
        ys.append(y_s + dskip_ref[:, s * LANES:(s + 1) * LANES] * bt_scr[s])
    y = jnp.concatenate(ys, axis=-1)
    y = 0.5 * y * (1.0 + lax.erf(y * (1.0 / math.sqrt(2.0))))
    zz = _dotb(y, wglu_ref[...]) + bglu_ref[...]
    out_ref[...] = (zz[:, :d] * _sigmoid(zz[:, d:])).reshape(nb, rt, d)


def _s5_mix(u4, layer, bmat, cmat, abr, abi, dskip, wglu, bglu, nb=8, rt=128):
    nslab, bsz, seq, _ = u4.shape
    assert rt % 8 == 0 and bmat.shape[1] == nslab
    ns = abr.shape[-1]
    d = wglu.shape[-1] // 2
    full = lambda arr: _layer_spec(arr, layer)
    params = (bmat, cmat, abr, abi, dskip, wglu, bglu)
    return pl.pallas_call(
        functools.partial(_s5_kernel, nslab=nslab),
        grid=(bsz // nb, seq // rt),
        in_specs=[pl.BlockSpec((nslab, nb, rt, LANES), lambda b, j: (0, b, j, 0))] + [full(p) for p in params],
        out_specs=pl.BlockSpec((nb, rt, d), lambda b, j: (b, j, 0)),
        out_shape=jax.ShapeDtypeStruct((bsz, seq, d), F32),
        scratch_shapes=[pltpu.VMEM((nb, ns), F32), pltpu.VMEM((nb, ns), F32),
                        pltpu.VMEM((nb * rt, 2 * ns // nslab), F32),
                        pltpu.VMEM((nslab, nb * rt, LANES), F32),
                        pltpu.VMEM((nslab, nb * rt, LANES), F32)],
        compiler_params=pltpu.CompilerParams(dimension_semantics=("parallel", "arbitrary"),
                                             vmem_limit_bytes=VMEM_LIMIT),
        name="s5_mix",
    )(u4, *params)


def _merge_ffn_kernel(x_ref, gt_ref, ya_ref, yb_ref, wout_ref, g2_ref, wup_ref, wdn_ref, gf_ref,
                      out_ref, *, final_norm, ff_chunk):
    d = x_ref.shape[1]
    gt = gt_ref[...]
    m = gt[:, :d] * ya_ref[...] + gt[:, d:] * yb_ref[...]
    x1 = x_ref[...] + _dotb(m, wout_ref[...])
    xn = x1 * lax.rsqrt(jnp.mean(x1 * x1, axis=-1, keepdims=True) + NORM_EPS) * g2_ref[...]
    xn = xn.astype(BF16)
    acc = x1
    dff = wup_ref.shape[1]
    for c in range(dff // ff_chunk):
        cs = slice(c * ff_chunk, (c + 1) * ff_chunk)
        h = jnp.dot(xn, wup_ref[:, cs], preferred_element_type=F32)
        h = jnp.square(jnp.maximum(h, 0.0))
        acc = acc + _dotb(h, wdn_ref[cs, :])
    if final_norm:
        acc = acc * lax.rsqrt(jnp.mean(acc * acc, axis=-1, keepdims=True) + NORM_EPS) * gf_ref[...]
    out_ref[...] = acc


def _merge_ffn(x2, gates, ya, yb, layer, wout, g2, wup, wdn, gf, final_norm, tm=512, ff_chunk=1024):
    t, d = x2.shape
    row = lambda w: pl.BlockSpec((tm, w), lambda i: (i, 0))
    full = lambda arr: _layer_spec(arr, layer)
    return pl.pallas_call(
        functools.partial(_merge_ffn_kernel, final_norm=final_norm, ff_chunk=ff_chunk),
        grid=(t // tm,),
        in_specs=[row(d), row(2 * d), row(d), row(d), full(wout), full(g2), full(wup), full(wdn),
                  pl.BlockSpec(gf.shape, lambda i: (0, 0))],
        out_specs=row(d),
        out_shape=jax.ShapeDtypeStruct((t, d), F32),
        compiler_params=pltpu.CompilerParams(dimension_semantics=("parallel",),
                                             vmem_limit_bytes=VMEM_LIMIT),
        name="merge_ffn",
    )(x2, gates, ya, yb, wout, g2, wup, wdn, gf)


def _s5_layer_params(a_re, a_im, log_dt, b_re, b_im, c_re, c_im):
    G, P, C = b_re.shape
    gps = LANES // C
    nslab = G // gps
    abr, abi, bbr, bbi = _s5_disc(a_re, a_im, log_dt, jnp.transpose(b_re, (2, 0, 1)),
                                  jnp.transpose(b_im, (2, 0, 1)))
    eye = jnp.eye(gps, dtype=F32)

    def bd_in(bb):
        bb = jnp.transpose(bb, (1, 0, 2)).reshape(nslab, gps, C, P)
        return jnp.einsum('sgcp,gh->sgchp', bb, eye).reshape(nslab, gps * C, gps * P)

    def bd_out(cc):
        cc = cc.reshape(nslab, gps, C, P)
        return jnp.einsum('sgcp,gh->sgphc', cc, eye).reshape(nslab, gps * P, gps * C)

    bmat = jnp.concatenate([bd_in(bbr), bd_in(bbi)], axis=2).astype(BF16)
    cmat = jnp.concatenate([bd_out(c_re), -bd_out(c_im)], axis=1).astype(BF16)
    return bmat, cmat, abr.reshape(1, G * P), abi.reshape(1, G * P)


def kernel(x, norm1_g, w_in, mu_shift, k_k, k_a, r_k, w0, w_decay_up, a0, w_aaa_up, w_gate_up,
           lnx_w, lnx_b, w_rwkv_proj, a_re, a_im, log_dt, b_re, b_im, c_re, c_im, d_skip, w_glu,
           b_glu, w_out, norm2_g, w_ff_up, w_ff_down, norm_f_g):
    bsz, seq, d = x.shape
    depth = w_in.shape[0]
    n_shift = mu_shift.shape[1]
    n_ssm = d_skip.shape[1]
    t = bsz * seq
    vec = lambda a: a.reshape(a.shape[0], 1, -1)
    bf = lambda a: a.astype(BF16)
    rwkv_params = (vec(k_k), vec(k_a), vec(r_k), vec(w0), bf(w_decay_up), vec(a0),
                   bf(w_aaa_up), bf(w_gate_up), vec(lnx_w), vec(lnx_b), bf(w_rwkv_proj))
    s5_params = jax.vmap(_s5_layer_params)(a_re, a_im, log_dt, b_re, b_im, c_re, c_im)
    s5_params += (vec(d_skip), bf(w_glu), vec(b_glu))
    w_in_bf, w_out_bf, w_up_bf, w_dn_bf = bf(w_in), bf(w_out), bf(w_ff_up), bf(w_ff_down)
    g1, g2, mu = vec(norm1_g), vec(norm2_g), vec(mu_shift)
    x2 = x.reshape(t, d)
    for l in range(depth):
        z_rwkv, u, gates = _in_proj(x2, seq, g1, w_in_bf, mu, l, n_shift, n_ssm)
        y_a = _rwkv_mix(z_rwkv, seq, l, *rwkv_params)
        y_b = _s5_mix(u.reshape(-1, bsz, seq, LANES), l, *s5_params)
        x2 = _merge_ffn(x2, gates, y_a, y_b.reshape(t, d), l, w_out_bf, g2, w_up_bf,
                        w_dn_bf, norm_f_g.reshape(1, -1), final_norm=(l == depth - 1))
    return x2.reshape(bsz, seq, d)
```
